```python
import math
import jax, jax.numpy as jnp
from jax import lax
import numpy as np

D_MODEL = 1024
BATCH = 8
SEQ = 4096
DEPTH = 2

CHUNK = 64
Q_BLOCK = 128
PLE_DIM = 256
D_FF = 2816
EPS = 1e-6
SSD_HEADS = 16
SSD_HEAD_DIM = 64
D_SSM = SSD_HEADS * SSD_HEAD_DIM
SSD_GROUPS = 2
SSD_HEADS_PER_GROUP = SSD_HEADS // SSD_GROUPS
SSD_STATE = 128
CONV_WIDTH = 4
CONV_DIM = D_SSM + 2 * SSD_GROUPS * SSD_STATE
MLA_HEADS = 8
QK_NOPE_DIM = 128
QK_ROPE_DIM = 64
V_HEAD_DIM = 128
Q_LORA_RANK = 384
KV_LORA_RANK = 256
D_MLA = MLA_HEADS * V_HEAD_DIM
D_MIX = D_SSM + D_MLA
ROPE_THETA = 10000.0
IN_WIDTHS = (D_SSM, CONV_DIM, SSD_HEADS, Q_LORA_RANK, KV_LORA_RANK, QK_ROPE_DIM)
D_IN_PROJ = D_SSM + CONV_DIM + SSD_HEADS + Q_LORA_RANK + KV_LORA_RANK + QK_ROPE_DIM

kernel_name = "hybrid_ssd_mla_macaron_ple"


def rms_norm(x, w):
    xf = x.astype(jnp.float32)
    y = xf * lax.rsqrt(jnp.mean(xf * xf, axis=-1, keepdims=True) + EPS)
    return (y * w.astype(jnp.float32)).astype(x.dtype)


def swiglu_ffn(x, w_in, w_out):
    g, u = jnp.split(x @ w_in, 2, axis=-1)
    return (jax.nn.silu(g) * u) @ w_out


def split_in_proj(zx):
    pieces, start = [], 0
    for w in IN_WIDTHS:
        pieces.append(zx[..., start:start + w])
        start += w
    return pieces


def causal_depthwise_conv(x, w, b):
    y = lax.conv_general_dilated(
        x, w[:, None, :].astype(x.dtype), window_strides=(1,),
        padding=[(CONV_WIDTH - 1, 0)], dimension_numbers=('NWC', 'WIO', 'NWC'),
        feature_group_count=x.shape[-1])
    return y + b


def rope_tables(positions):
    inv = ROPE_THETA ** (-jnp.arange(0, QK_ROPE_DIM, 2, dtype=jnp.float32) / QK_ROPE_DIM)
    ang = positions.astype(jnp.float32)[..., None] * inv
    return jnp.cos(ang), jnp.sin(ang)


def apply_rope(x, cos, sin):
    xf = x.astype(jnp.float32)
    x1, x2 = jnp.split(xf, 2, axis=-1)
    return jnp.concatenate([x1 * cos - x2 * sin, x2 * cos + x1 * sin], axis=-1).astype(x.dtype)


def ssd_mixer(z, xbc_raw, dt_raw, conv_w, conv_b, dt_bias, a_log, d_skip, norm_w):
    b, s, _ = z.shape
    nc = s // CHUNK
    G, HG, P, N = SSD_GROUPS, SSD_HEADS_PER_GROUP, SSD_HEAD_DIM, SSD_STATE
    xbc = jax.nn.silu(causal_depthwise_conv(xbc_raw, conv_w, conv_b)).astype(jnp.float32)
    xs = xbc[..., :D_SSM].reshape(b, nc, CHUNK, G, HG, P)
    Bm = xbc[..., D_SSM:D_SSM + G * N].reshape(b, nc, CHUNK, G, N)
    Cm = xbc[..., D_SSM + G * N:].reshape(b, nc, CHUNK, G, N)
    dt = jax.nn.softplus(dt_raw.astype(jnp.float32) + dt_bias.astype(jnp.float32))
    dt = dt.reshape(b, nc, CHUNK, G, HG)
    A = -jnp.exp(a_log.astype(jnp.float32)).reshape(G, HG)
    a_cum = jnp.cumsum(dt * A, axis=2)
    xdt = xs * dt[..., None]
    seg = a_cum[:, :, :, None] - a_cum[:, :, None, :]
    tril = jnp.tril(jnp.ones((CHUNK, CHUNK), dtype=bool))[:, :, None, None]
    decay = jnp.exp(jnp.where(tril, seg, -jnp.inf))
    cb = jnp.einsum('bclgn,bcsgn->bclsg', Cm, Bm)
    y_diag = jnp.einsum('bclsgh,bcsghp->bclghp', cb[..., None] * decay, xdt)
    decay_to_end = jnp.exp(a_cum[:, :, -1:] - a_cum)
    states = jnp.einsum('bclgn,bclghp->bcghpn', Bm, xdt * decay_to_end[..., None])
    chunk_decay = jnp.exp(a_cum[:, :, -1])

    def step(h, inp):
        st, dec = inp
        return h * dec[..., None, None] + st, h

    h0 = jnp.zeros((b, G, HG, P, N), jnp.float32)
    _, prev = lax.scan(step, h0, (jnp.moveaxis(states, 1, 0), jnp.moveaxis(chunk_decay, 1, 0)))
    prev = jnp.moveaxis(prev, 0, 1)
    y_off = jnp.einsum('bclgn,bcghpn->bclghp', Cm, prev) * jnp.exp(a_cum)[..., None]
    y = y_diag + y_off + xs * d_skip.astype(jnp.float32).reshape(G, HG)[..., None]
    y = y.reshape(b, s, D_SSM) * jax.nn.silu(z.astype(jnp.float32))
    yg = y.reshape(b, s, G, D_SSM // G)
    yg = yg * lax.rsqrt(jnp.mean(yg * yg, axis=-1, keepdims=True) + EPS)
    return (yg.reshape(b, s, D_SSM) * norm_w.astype(jnp.float32)).astype(z.dtype)


def mla_mixer(q_a, kv_a, k_rope_raw, cos, sin, q_a_norm, w_q_b, kv_a_norm, w_kv_b):
    b, s, _ = q_a.shape
    q = (rms_norm(q_a, q_a_norm) @ w_q_b).reshape(b, s, MLA_HEADS, QK_NOPE_DIM + QK_ROPE_DIM)
    q_nope = q[..., :QK_NOPE_DIM]
    q_rope = apply_rope(q[..., QK_NOPE_DIM:], cos[:, :, None], sin[:, :, None])
    kv = (rms_norm(kv_a, kv_a_norm) @ w_kv_b).reshape(b, s, MLA_HEADS, QK_NOPE_DIM + V_HEAD_DIM)
    k_nope, v = kv[..., :QK_NOPE_DIM], kv[..., QK_NOPE_DIM:]
    k_rope = apply_rope(k_rope_raw, cos, sin)
    scale = (QK_NOPE_DIM + QK_ROPE_DIM) ** -0.5
    outs = []
    for qb in range(s // Q_BLOCK):
        q0 = qb * Q_BLOCK
        k_end = q0 + Q_BLOCK
        sc = (jnp.einsum('bqhd,bkhd->bhqk', q_nope[:, q0:k_end], k_nope[:, :k_end])
              + jnp.einsum('bqhr,bkr->bhqk', q_rope[:, q0:k_end], k_rope[:, :k_end]))
        sc = sc.astype(jnp.float32) * scale
        q_chunk = (q0 + jnp.arange(Q_BLOCK)) // CHUNK
        k_chunk = jnp.arange(k_end) // CHUNK
        mask = k_chunk[None, :] <= q_chunk[:, None]
        probs = jax.nn.softmax(jnp.where(mask, sc, -jnp.inf), axis=-1).astype(v.dtype)
        outs.append(jnp.einsum('bhqk,bkhd->bqhd', probs, v[:, :k_end]))
    return jnp.concatenate(outs, axis=1).reshape(b, s, D_MLA)


def _normal(k, shape, fan_in):
    return jax.random.normal(k, shape, jnp.float32) * fan_in ** -0.5


def _gain(k, shape):
    return 1.0 + 0.02 * jax.random.normal(k, shape, jnp.float32)


def setup_inputs(seed: int = 0) -> dict:
    key = jax.random.key(seed)
    ks = jax.random.split(key, 32)
    L = DEPTH
    x = jax.random.normal(ks[0], (BATCH, SEQ, D_MODEL), jnp.float32)
    p = jax.random.normal(ks[1], (DEPTH, BATCH, SEQ, PLE_DIM), jnp.float32)
    offset = jax.random.randint(ks[2], (BATCH, 1), 0, 65536, dtype=jnp.int32)
    positions = (offset + jnp.arange(SEQ, dtype=jnp.int32)[None, :]).astype(jnp.int32)
    dt_init = jnp.exp(jax.random.uniform(ks[3], (L, SSD_HEADS), jnp.float32,
                                         minval=math.log(1e-3), maxval=math.log(1e-1)))
    dt_bias = dt_init + jnp.log(-jnp.expm1(-dt_init))
    a_log = jnp.log(jax.random.uniform(ks[4], (L, SSD_HEADS), jnp.float32, minval=1.0, maxval=16.0))
    return {
        "x": x,
        "p": p,
        "positions": positions,
        "ffn1_norm": _gain(ks[5], (L, D_MODEL)),
        "ffn1_w_in": _normal(ks[6], (L, D_MODEL, 2 * D_FF), D_MODEL),
        "ffn1_w_out": _normal(ks[7], (L, D_FF, D_MODEL), D_FF),
        "mix_norm": _gain(ks[8], (L, D_MODEL)),
        "w_in_mix": _normal(ks[9], (L, D_MODEL, D_IN_PROJ), D_MODEL),
        "conv_w": _normal(ks[10], (L, CONV_WIDTH, CONV_DIM), CONV_WIDTH),
        "conv_b": 0.02 * jax.random.normal(ks[11], (L, CONV_DIM), jnp.float32),
        "dt_bias": dt_bias,
        "a_log": a_log,
        "d_skip": 1.0 + 0.1 * jax.random.normal(ks[12], (L, SSD_HEADS), jnp.float32),
        "ssd_norm": _gain(ks[13], (L, D_SSM)),
        "q_a_norm": _gain(ks[14], (L, Q_LORA_RANK)),
        "w_q_b": _normal(ks[15], (L, Q_LORA_RANK, MLA_HEADS * (QK_NOPE_DIM + QK_ROPE_DIM)), Q_LORA_RANK),
        "kv_a_norm": _gain(ks[16], (L, KV_LORA_RANK)),
        "w_kv_b": _normal(ks[17], (L, KV_LORA_RANK, MLA_HEADS * (QK_NOPE_DIM + V_HEAD_DIM)), KV_LORA_RANK),
        "w_out_mix": _normal(ks[18], (L, D_MIX, D_MODEL), D_MIX),
        "ffn2_norm": _gain(ks[19], (L, D_MODEL)),
        "ffn2_w_in": _normal(ks[20], (L, D_MODEL, 2 * D_FF), D_MODEL),
        "ffn2_w_out": _normal(ks[21], (L, D_FF, D_MODEL), D_FF),
        "ple_norm": _gain(ks[22], (L, D_MODEL)),
        "w_ple_gate": _normal(ks[23], (L, D_MODEL, D_MODEL), D_MODEL),
        "w_ple_proj": _normal(ks[24], (L, PLE_DIM, D_MODEL), PLE_DIM),
        "final_norm": _gain(ks[25], (D_MODEL,)),
    }


def reference(x, p, positions, ffn1_norm, ffn1_w_in, ffn1_w_out, mix_norm, w_in_mix,
              conv_w, conv_b, dt_bias, a_log, d_skip, ssd_norm, q_a_norm, w_q_b,
              kv_a_norm, w_kv_b, w_out_mix, ffn2_norm, ffn2_w_in, ffn2_w_out,
              ple_norm, w_ple_gate, w_ple_proj, final_norm):
    cos, sin = rope_tables(positions)
    h = x
    for i in range(DEPTH):
        h = h + 0.5 * swiglu_ffn(rms_norm(h, ffn1_norm[i]), ffn1_w_in[i], ffn1_w_out[i])
        u = rms_norm(h, mix_norm[i])
        z, xbc, dt_raw, q_a, kv_a, k_rope_raw = split_in_proj(u @ w_in_mix[i])
        y_ssd = ssd_mixer(z, xbc, dt_raw, conv_w[i], conv_b[i], dt_bias[i], a_log[i],
                          d_skip[i], ssd_norm[i])
        y_mla = mla_mixer(q_a, kv_a, k_rope_raw, cos, sin, q_a_norm[i], w_q_b[i],
                          kv_a_norm[i], w_kv_b[i])
        h = h + jnp.concatenate([y_ssd, y_mla.astype(y_ssd.dtype)], axis=-1) @ w_out_mix[i]
        h = h + 0.5 * swiglu_ffn(rms_norm(h, ffn2_norm[i]), ffn2_w_in[i], ffn2_w_out[i])
        gate = jax.nn.sigmoid(rms_norm(h, ple_norm[i]) @ w_ple_gate[i])
        h = h + gate * (p[i] @ w_ple_proj[i])
    return rms_norm(h, final_norm)
```

```python
import functools

import jax
import jax.numpy as jnp
from jax import lax
from jax.experimental import pallas as pl
from jax.experimental.pallas import tpu as pltpu

F32 = jnp.float32
BF16 = jnp.bfloat16

D_MODEL = 1024
CHUNK = 64
PLE_DIM = 256
D_FF = 2816
EPS = 1e-6
SSD_HEADS = 16
SSD_HEAD_DIM = 64
D_SSM = SSD_HEADS * SSD_HEAD_DIM
SSD_GROUPS = 2
SSD_HEADS_PER_GROUP = SSD_HEADS // SSD_GROUPS
SSD_STATE = 128
CONV_WIDTH = 4
CONV_DIM = D_SSM + 2 * SSD_GROUPS * SSD_STATE
MLA_HEADS = 8
QK_NOPE_DIM = 128
QK_ROPE_DIM = 64
V_HEAD_DIM = 128
Q_LORA_RANK = 384
KV_LORA_RANK = 256
D_MLA = MLA_HEADS * V_HEAD_DIM
ROPE_THETA = 10000.0
SOFTMAX_SCALE = (QK_NOPE_DIM + QK_ROPE_DIM) ** -0.5

V7X_LANES = 128
V7X_SUBLANES = 8
V7X_MXU_DIM = 256
V7X_VMEM_LIMIT_BYTES = 56 * 1024 * 1024

QK_PAD = V7X_MXU_DIM
ROPE_LO = QK_NOPE_DIM
ROPE_HALF = QK_ROPE_DIM // 2
DT_PAD = V7X_LANES
FF_CHUNK = V7X_MXU_DIM
SSD_BLOCK = 256
CONV_HIST = V7X_SUBLANES
TOKEN_TILE = 256
ATTN_TQ = 512
ATTN_TK = 256
NEG_BIG = -1e30

_C_Z = 0
_C_XBC = _C_Z + D_SSM
_C_DT = _C_XBC + CONV_DIM
_C_QA = _C_DT + DT_PAD
_C_KVA = _C_QA + Q_LORA_RANK
_C_KRA = _C_KVA + KV_LORA_RANK
_C_KRB = _C_KRA + QK_PAD
_C_END = _C_KRB + QK_PAD


def _dot(a, b):
    return jnp.dot(a, b, preferred_element_type=F32)


def _rms(x, w):
    return x * lax.rsqrt(jnp.mean(x * x, axis=-1, keepdims=True) + EPS) * w


def _ffn(xn, wg_ref, wu_ref, wo_ref):
    acc = None
    for c in range(D_FF // FF_CHUNK):
        sl = slice(c * FF_CHUNK, (c + 1) * FF_CHUNK)
        g = _dot(xn, wg_ref[:, sl])
        u = _dot(xn, wu_ref[:, sl])
        a = (g * jax.nn.sigmoid(g) * u).astype(BF16)
        d = _dot(a, wo_ref[sl, :])
        acc = d if acc is None else acc + d
    return acc


def _const_spec(shape, grid_rank):
    zeros = (0,) * len(shape)
    if grid_rank == 1:
        index_map = lambda i: zeros
    elif grid_rank == 2:
        index_map = lambda i, j: zeros
    else:
        index_map = lambda i, j, k: zeros
    return pl.BlockSpec(shape, index_map, pipeline_mode=pl.Buffered(1))


def _row_spec(tm, width):
    return pl.BlockSpec((tm, width), lambda i: (i, 0))


def _rope_kernel(pos_ref, inv_ref, cmask_ref, sgn_ref, ck_ref, sk_ref):
    ang = pos_ref[...].astype(F32) * inv_ref[...]
    ck_ref[...] = jnp.cos(ang) * cmask_ref[...]
    sk_ref[...] = jnp.sin(ang) * sgn_ref[...]


def _rope_tables(positions):
    t = positions.size
    tm = min(t, 1024)
    inv = ROPE_THETA ** (-jnp.arange(0, QK_ROPE_DIM, 2, dtype=F32) / QK_ROPE_DIM)
    zeros_lo = jnp.zeros((ROPE_LO,), F32)
    zeros_hi = jnp.zeros((QK_PAD - ROPE_LO - QK_ROPE_DIM,), F32)
    ones_h = jnp.ones((ROPE_HALF,), F32)
    inv_row = jnp.concatenate([zeros_lo, inv, inv, zeros_hi])[None, :]
    cmask = jnp.concatenate([zeros_lo, ones_h, ones_h, zeros_hi])[None, :]
    sgn = jnp.concatenate([zeros_lo, -ones_h, ones_h, zeros_hi])[None, :]
    row = _const_spec((1, QK_PAD), 1)
    return pl.pallas_call(
        _rope_kernel,
        grid=(t // tm,),
        in_specs=[pl.BlockSpec((tm, 1), lambda i: (i, 0)), row, row, row],
        out_specs=[_row_spec(tm, QK_PAD), _row_spec(tm, QK_PAD)],
        out_shape=[jax.ShapeDtypeStruct((t, QK_PAD), F32)] * 2,
        compiler_params=pltpu.CompilerParams(dimension_semantics=("parallel",)),
        name="rope_tables",
    )(positions.reshape(t, 1), inv_row, cmask, sgn)


def _pre_kernel(x_ref, n1_ref, wg_ref, wu_ref, wo_ref, nm_ref, win_ref, qn_ref, kn_ref,
                wq1_ref, wq2_ref, wk_ref, wv_ref, ck_ref, sk_ref,
                h_ref, z_ref, xbc_ref, dt_ref, q_ref, k_ref, v_ref):
    x = x_ref[...]
    xn = _rms(x, n1_ref[...]).astype(BF16)
    h1 = x + 0.5 * _ffn(xn, wg_ref, wu_ref, wo_ref)
    h_ref[...] = h1
    u = _rms(h1, nm_ref[...]).astype(BF16)
    z_ref[...] = _dot(u, win_ref[:, _C_Z:_C_XBC])
    xbc_ref[...] = _dot(u, win_ref[:, _C_XBC:_C_DT])
    dt_ref[...] = _dot(u, win_ref[:, _C_DT:_C_QA])
    qa = _dot(u, win_ref[:, _C_QA:_C_KVA])
    kva = _dot(u, win_ref[:, _C_KVA:_C_KRA])
    kra = _dot(u, win_ref[:, _C_KRA:_C_KRB])
    krb = _dot(u, win_ref[:, _C_KRB:_C_END])
    qn = _rms(qa, qn_ref[...]).astype(BF16)
    kvn = _rms(kva, kn_ref[...]).astype(BF16)
    ck = ck_ref[...]
    sk = sk_ref[...]
    kr = kra * ck + krb * sk
    lane = lax.broadcasted_iota(jnp.int32, (1, QK_PAD), 1)
    cq = (ck + (lane < ROPE_LO).astype(F32)) * SOFTMAX_SCALE
    sq = sk * SOFTMAX_SCALE
    for h in range(MLA_HEADS):
        sl = slice(h * QK_PAD, (h + 1) * QK_PAD)
        q_ref[:, sl] = (_dot(qn, wq1_ref[:, sl]) * cq + _dot(qn, wq2_ref[:, sl]) * sq).astype(BF16)
        k_ref[:, sl] = (_dot(kvn, wk_ref[:, sl]) + kr).astype(BF16)
    v_ref[...] = _dot(kvn, wv_ref[...]).astype(BF16)


def _pre_call(h, n1, wg, wu, wo, nm, win, qn, kn, wq1, wq2, wk, wv, ck, sk):
    t = h.shape[0]
    tm = min(t, TOKEN_TILE)
    consts = [n1, wg, wu, wo, nm, win, qn, kn, wq1, wq2, wk, wv]
    in_specs = ([_row_spec(tm, D_MODEL)] + [_const_spec(c.shape, 1) for c in consts]
                + [_row_spec(tm, QK_PAD), _row_spec(tm, QK_PAD)])
    widths = [(D_MODEL, F32), (D_SSM, F32), (CONV_DIM, F32), (DT_PAD, F32),
              (MLA_HEADS * QK_PAD, BF16), (MLA_HEADS * QK_PAD, BF16), (D_MLA, BF16)]
    return pl.pallas_call(
        _pre_kernel,
        grid=(t // tm,),
        in_specs=in_specs,
        out_specs=[_row_spec(tm, w) for w, _ in widths],
        out_shape=[jax.ShapeDtypeStruct((t, w), d) for w, d in widths],
        compiler_params=pltpu.CompilerParams(
            dimension_semantics=("parallel",), vmem_limit_bytes=V7X_VMEM_LIMIT_BYTES),
        name="pre",
    )(h, *consts, ck, sk)


def _split_dot(v, m, parts, left=False):
    out = None
    rem = v
    for _ in range(parts):
        piece = rem.astype(BF16)
        rem = rem - piece.astype(F32)
        d = _dot(m, piece) if left else _dot(piece, m)
        out = d if out is None else out + d
    return out


def _softplus(x):
    return jnp.maximum(x, 0.0) + jnp.log1p(jnp.exp(-jnp.abs(x)))


def _ssd_kernel(xbc_ref, z_ref, dt_ref, cw_ref, cb_ref, dtb_ref, alog_ref, dskip_ref, nw_ref,
                tri_ref, exp_ref, y_ref, state_ref, xpad_ref):
    q = SSD_BLOCK
    hp = SSD_HEADS_PER_GROUP * SSD_HEAD_DIM

    @pl.when(pl.program_id(1) == 0)
    def _():
        state_ref[...] = jnp.zeros_like(state_ref)
        xpad_ref[0:CONV_HIST, :] = jnp.zeros((CONV_HIST, CONV_DIM), F32)

    xr = xbc_ref[...]
    xpad_ref[CONV_HIST:CONV_HIST + q, :] = xr
    conv = cb_ref[...]
    for k in range(CONV_WIDTH):
        start = CONV_HIST - (CONV_WIDTH - 1) + k
        conv = conv + cw_ref[k:k + 1, :] * xpad_ref[start:start + q, :]
    xpad_ref[0:CONV_HIST, :] = xr[q - CONV_HIST:q, :]
    xbc = conv * jax.nn.sigmoid(conv)
    xs = xbc[:, :D_SSM]

    lane = lax.broadcasted_iota(jnp.int32, (1, DT_PAD), 1)
    head_lane = lane < SSD_HEADS
    dt = jnp.where(head_lane, _softplus(dt_ref[...] + dtb_ref[...]), 0.0)
    a_neg = jnp.where(head_lane, -jnp.exp(alog_ref[...]), 0.0)
    a_cum = _split_dot(dt * a_neg, tri_ref[...], 3, left=True)
    a_cum_t = a_cum.T
    a_last = a_cum[q - 1:q, :]
    to_end = jnp.exp(a_last - a_cum)
    from_start = jnp.exp(a_cum)

    expand = exp_ref[...]
    dt_e = _split_dot(dt, expand, 2)
    w_e = _split_dot(dt * to_end, expand, 2)
    fs_e = _split_dot(from_start, expand, 2)
    xdt = (xs * dt_e).astype(BF16)
    xw = (xs * w_e).astype(BF16)

    row = lax.broadcasted_iota(jnp.int32, (q, q), 0)
    col = lax.broadcasted_iota(jnp.int32, (q, q), 1)
    causal = col <= row
    lane_pair = lax.broadcasted_iota(jnp.int32, (1, 2 * SSD_HEAD_DIM), 1)
    first_of_pair = lane_pair < SSD_HEAD_DIM

    z = z_ref[...]
    for g in range(SSD_GROUPS):
        b_g = xbc[:, D_SSM + g * SSD_STATE:D_SSM + (g + 1) * SSD_STATE]
        c_g = xbc[:, D_SSM + (SSD_GROUPS + g) * SSD_STATE:D_SSM + (SSD_GROUPS + g + 1) * SSD_STATE]
        b_bf = b_g.astype(BF16)
        c_bf = c_g.astype(BF16)
        cb = lax.dot_general(c_bf, b_bf, (((1,), (1,)), ((), ())), preferred_element_type=F32)
        gsl = slice(g * hp, (g + 1) * hp)
        state = state_ref[g]
        y_g = _dot(c_bf, state.astype(BF16)) * fs_e[:, gsl] + xs[:, gsl] * dskip_ref[:, gsl]
        pieces = []
        for pair in range(SSD_HEADS_PER_GROUP // 2):
            outs = []
            for sub in range(2):
                h = g * SSD_HEADS_PER_GROUP + 2 * pair + sub
                seg = a_cum[:, h:h + 1] - a_cum_t[h:h + 1, :]
                m = (cb * jnp.exp(jnp.where(causal, seg, NEG_BIG))).astype(BF16)
                psl = slice(g * hp + pair * 2 * SSD_HEAD_DIM, g * hp + (pair + 1) * 2 * SSD_HEAD_DIM)
                outs.append(_dot(m, xdt[:, psl]))
            pieces.append(jnp.where(first_of_pair, outs[0], outs[1]))
        y_g = y_g + jnp.concatenate(pieces, axis=-1)
        y_g = y_g * (z[:, gsl] * jax.nn.sigmoid(z[:, gsl]))
        y_g = y_g * lax.rsqrt(jnp.mean(y_g * y_g, axis=-1, keepdims=True) + EPS) * nw_ref[:, gsl]
        y_ref[:, gsl] = y_g.astype(y_ref.dtype)
        state_ref[g] = state * fs_e[q - 1:q, gsl] + _dot(b_g.T.astype(BF16), xw[:, gsl])


def _ssd_call(xbc, z, dtm, cw, cb, dtb, alog, dskip, nw, tri, expand, batch, seq):
    q = SSD_BLOCK
    nblk = seq // q
    hp = SSD_HEADS_PER_GROUP * SSD_HEAD_DIM
    blk = lambda w: pl.BlockSpec((q, w), lambda b, c: (b * nblk + c, 0))
    consts = [cw, cb, dtb, alog, dskip, nw, tri, expand]
    return pl.pallas_call(
        _ssd_kernel,
        grid=(batch, nblk),
        in_specs=[blk(CONV_DIM), blk(D_SSM), blk(DT_PAD)] + [_const_spec(c.shape, 2) for c in consts],
        out_specs=blk(D_SSM),
        out_shape=jax.ShapeDtypeStruct((batch * seq, D_SSM), BF16),
        scratch_shapes=[pltpu.VMEM((SSD_GROUPS, SSD_STATE, hp), F32),
                        pltpu.VMEM((CONV_HIST + q, CONV_DIM), F32)],
        compiler_params=pltpu.CompilerParams(
            dimension_semantics=("arbitrary", "arbitrary"), vmem_limit_bytes=V7X_VMEM_LIMIT_BYTES),
        name="ssd",
    )(xbc, z, dtm, *consts)


def _attn_kernel(q_ref, k_ref, v_ref, o_ref):
    tq, tk = q_ref.shape[0], ATTN_TK
    qi = pl.program_id(2)
    q = q_ref[...]

    def step(j, carry, mask):
        m, l, acc = carry
        start = pl.multiple_of(j * tk, tk)
        k = k_ref[pl.ds(start, tk), :]
        v = v_ref[pl.ds(start, tk), :]
        s = lax.dot_general(q, k, (((1,), (1,)), ((), ())), preferred_element_type=F32)
        if mask is not None:
            s = jnp.where(mask, s, NEG_BIG)
        m_new = jnp.maximum(m, jnp.max(s, axis=-1, keepdims=True))
        alpha = jnp.exp(m - m_new)
        p = jnp.exp(s - m_new)
        l = alpha * l + jnp.sum(p, axis=-1, keepdims=True)
        acc = alpha * acc + _dot(p.astype(BF16), v)
        return m_new, l, acc

    carry = (jnp.full((tq, 1), NEG_BIG, F32), jnp.zeros((tq, 1), F32),
             jnp.zeros((tq, V_HEAD_DIM), F32))
    row_chunk = lax.broadcasted_iota(jnp.int32, (tq, tk), 0) // CHUNK
    col = lax.broadcasted_iota(jnp.int32, (tq, tk), 1)
    nd = tq // tk
    for d in range(nd):
        mask = (col + d * tk) // CHUNK <= row_chunk
        carry = step(qi * nd + d, carry, mask)
    carry = lax.fori_loop(0, qi * nd, lambda j, c: step(j, c, None), carry)
    _, l, acc = carry
    o_ref[...] = (acc / l).astype(o_ref.dtype)


def _attn_call(q, k, v, batch, seq):
    tq = min(seq, ATTN_TQ)
    nq = seq // tq
    return pl.pallas_call(
        _attn_kernel,
        grid=(batch, MLA_HEADS, nq),
        in_specs=[pl.BlockSpec((tq, QK_PAD), lambda b, h, i: (b * nq + i, h)),
                  pl.BlockSpec((seq, QK_PAD), lambda b, h, i: (b, h)),
                  pl.BlockSpec((seq, V_HEAD_DIM), lambda b, h, i: (b, h))],
        out_specs=pl.BlockSpec((tq, V_HEAD_DIM), lambda b, h, i: (b * nq + i, h)),
        out_shape=jax.ShapeDtypeStruct((batch * seq, D_MLA), BF16),
        compiler_params=pltpu.CompilerParams(
            dimension_semantics=("parallel", "parallel", "arbitrary"),
            vmem_limit_bytes=V7X_VMEM_LIMIT_BYTES),
        name="attn",
    )(q, k, v)


def _post_kernel(h_ref, ys_ref, ym_ref, p_ref, wos_ref, wom_ref, n2_ref, wg_ref, wu_ref, wo_ref,
                 np_ref, wgate_ref, wproj_ref, nf_ref, o_ref, *, final):
    h2 = h_ref[...] + _dot(ys_ref[...], wos_ref[...]) + _dot(ym_ref[...], wom_ref[...])
    xn = _rms(h2, n2_ref[...]).astype(BF16)
    h3 = h2 + 0.5 * _ffn(xn, wg_ref, wu_ref, wo_ref)
    gate = jax.nn.sigmoid(_dot(_rms(h3, np_ref[...]).astype(BF16), wgate_ref[...]))
    h4 = h3 + gate * _dot(p_ref[...].astype(BF16), wproj_ref[...])
    if final:
        h4 = _rms(h4, nf_ref[...])
    o_ref[...] = h4


def _post_call(h, ys, ym, p, wos, wom, n2, wg, wu, wo, npl, wgate, wproj, nf, final):
    t = h.shape[0]
    tm = min(t, TOKEN_TILE)
    consts = [wos, wom, n2, wg, wu, wo, npl, wgate, wproj, nf]
    in_specs = ([_row_spec(tm, D_MODEL), _row_spec(tm, D_SSM), _row_spec(tm, D_MLA),
                 _row_spec(tm, PLE_DIM)] + [_const_spec(c.shape, 1) for c in consts])
    return pl.pallas_call(
        functools.partial(_post_kernel, final=final),
        grid=(t // tm,),
        in_specs=in_specs,
        out_specs=_row_spec(tm, D_MODEL),
        out_shape=jax.ShapeDtypeStruct((t, D_MODEL), F32),
        compiler_params=pltpu.CompilerParams(
            dimension_semantics=("parallel",), vmem_limit_bytes=V7X_VMEM_LIMIT_BYTES),
        name="post",
    )(h, ys, ym, p, *consts)


def _pad_cols(w, lo, width):
    return jnp.pad(w, ((0, 0), (lo, width - lo - w.shape[1])))


def _swap_halves(w):
    return jnp.concatenate([w[:, ROPE_HALF:], w[:, :ROPE_HALF]], axis=1)


def _layout_in_proj(w):
    o = 0
    w_z = w[:, o:o + D_SSM]; o += D_SSM
    w_xbc = w[:, o:o + CONV_DIM]; o += CONV_DIM
    w_dt = w[:, o:o + SSD_HEADS]; o += SSD_HEADS
    w_qa = w[:, o:o + Q_LORA_RANK]; o += Q_LORA_RANK
    w_kva = w[:, o:o + KV_LORA_RANK]; o += KV_LORA_RANK
    w_kr = w[:, o:o + QK_ROPE_DIM]
    return jnp.concatenate([
        w_z, w_xbc, _pad_cols(w_dt, 0, DT_PAD), w_qa, w_kva,
        _pad_cols(w_kr, ROPE_LO, QK_PAD), _pad_cols(_swap_halves(w_kr), ROPE_LO, QK_PAD)],
        axis=1).astype(BF16)


def _layout_q_b(w):
    w = w.reshape(Q_LORA_RANK, MLA_HEADS, QK_NOPE_DIM + QK_ROPE_DIM)
    plain, swapped = [], []
    for h in range(MLA_HEADS):
        wh = w[:, h, :]
        plain.append(_pad_cols(wh, 0, QK_PAD))
        swapped.append(_pad_cols(_swap_halves(wh[:, QK_NOPE_DIM:]), ROPE_LO, QK_PAD))
    return (jnp.concatenate(plain, axis=1).astype(BF16),
            jnp.concatenate(swapped, axis=1).astype(BF16))


def _layout_kv_b(w):
    w = w.reshape(KV_LORA_RANK, MLA_HEADS, QK_NOPE_DIM + V_HEAD_DIM)
    wk = jnp.concatenate([_pad_cols(w[:, h, :QK_NOPE_DIM], 0, QK_PAD) for h in range(MLA_HEADS)], axis=1)
    wv = w[:, :, QK_NOPE_DIM:].reshape(KV_LORA_RANK, D_MLA)
    return wk.astype(BF16), wv.astype(BF16)


def _row(v):
    return v.reshape(1, -1).astype(F32)


def kernel(x, p, positions, ffn1_norm, ffn1_w_in, ffn1_w_out, mix_norm, w_in_mix, conv_w, conv_b,
           dt_bias, a_log, d_skip, ssd_norm, q_a_norm, w_q_b, kv_a_norm, w_kv_b, w_out_mix,
           ffn2_norm, ffn2_w_in, ffn2_w_out, ple_norm, w_ple_gate, w_ple_proj, final_norm):
    batch, seq, _ = x.shape
    depth = p.shape[0]
    t = batch * seq
    assert seq % SSD_BLOCK == 0 and seq % ATTN_TK == 0 and t % min(t, TOKEN_TILE) == 0

    ck, sk = _rope_tables(positions)
    q_idx = jnp.arange(SSD_BLOCK)
    tri = (q_idx[None, :] <= q_idx[:, None]).astype(BF16)
    expand = (jnp.arange(D_SSM)[None, :] // SSD_HEAD_DIM == jnp.arange(DT_PAD)[:, None]).astype(BF16)

    h = x.reshape(t, D_MODEL)
    for i in range(depth):
        wq1, wq2 = _layout_q_b(w_q_b[i])
        wk, wv = _layout_kv_b(w_kv_b[i])
        h, z, xbc, dtm, q, k, v = _pre_call(
            h, _row(ffn1_norm[i]),
            ffn1_w_in[i][:, :D_FF].astype(BF16), ffn1_w_in[i][:, D_FF:].astype(BF16),
            ffn1_w_out[i].astype(BF16), _row(mix_norm[i]), _layout_in_proj(w_in_mix[i]),
            _row(q_a_norm[i]), _row(kv_a_norm[i]), wq1, wq2, wk, wv, ck, sk)
        y_ssd = _ssd_call(
            xbc, z, dtm, conv_w[i].astype(F32), _row(conv_b[i]),
            _pad_cols(_row(dt_bias[i]), 0, DT_PAD), _pad_cols(_row(a_log[i]), 0, DT_PAD),
            _row(jnp.repeat(d_skip[i], SSD_HEAD_DIM)), _row(ssd_norm[i]), tri, expand, batch, seq)
        y_mla = _attn_call(q, k, v, batch, seq)
        h = _post_call(
            h, y_ssd, y_mla, p[i].reshape(t, PLE_DIM),
            w_out_mix[i][:D_SSM].astype(BF16), w_out_mix[i][D_SSM:].astype(BF16),
            _row(ffn2_norm[i]),
            ffn2_w_in[i][:, :D_FF].astype(BF16), ffn2_w_in[i][:, D_FF:].astype(BF16),
            ffn2_w_out[i].astype(BF16), _row(ple_norm[i]), w_ple_gate[i].astype(BF16),
            w_ple_proj[i].astype(BF16), _row(final_norm), final=(i == depth - 1))
    return h.reshape(batch, seq, D_MODEL)
```

```python
import functools

import jax
import jax.numpy as jnp
from jax import lax
from jax.experimental import pallas as pl
from jax.experimental.pallas import tpu as pltpu

F32 = jnp.float32
BF16 = jnp.bfloat16

D_MODEL = 1024
CHUNK = 64
PLE_DIM = 256
D_FF = 2816
EPS = 1e-6
SSD_HEADS = 16
SSD_HEAD_DIM = 64
D_SSM = SSD_HEADS * SSD_HEAD_DIM
SSD_GROUPS = 2
SSD_HEADS_PER_GROUP = SSD_HEADS // SSD_GROUPS
SSD_STATE = 128
CONV_WIDTH = 4
CONV_DIM = D_SSM + 2 * SSD_GROUPS * SSD_STATE
MLA_HEADS = 8
QK_NOPE_DIM = 128
QK_ROPE_DIM = 64
V_HEAD_DIM = 128
Q_LORA_RANK = 384
KV_LORA_RANK = 256
D_MLA = MLA_HEADS * V_HEAD_DIM
ROPE_THETA = 10000.0
SOFTMAX_SCALE = (QK_NOPE_DIM + QK_ROPE_DIM) ** -0.5
LOG2_E = 1.4426950408889634

V7X_LANES = 128
V7X_SUBLANES = 8
V7X_MXU_DIM = 256
V7X_VMEM_LIMIT_BYTES = 56 * 1024 * 1024

QK_PAD = V7X_MXU_DIM
ROPE_LO = QK_NOPE_DIM
ROPE_HALF = QK_ROPE_DIM // 2
DT_PAD = V7X_LANES
FF_CHUNK = V7X_MXU_DIM
SSD_BLOCK = 256
CONV_HIST = V7X_SUBLANES
TOKEN_TILE = 256
ATTN_TQ = 512
ATTN_TK = 256
ATTN_HEADS_PER_STEP = 4
NEG_BIG = -1e30

_C_Z = 0
_C_XBC = _C_Z + D_SSM
_C_DT = _C_XBC + CONV_DIM
_C_QA = _C_DT + DT_PAD
_C_KVA = _C_QA + Q_LORA_RANK
_C_KRA = _C_KVA + KV_LORA_RANK
_C_KRB = _C_KRA + QK_PAD
_C_END = _C_KRB + QK_PAD


def _dot(a, b):
    return jnp.dot(a, b, preferred_element_type=F32)


def _rms(x, w):
    return x * lax.rsqrt(jnp.mean(x * x, axis=-1, keepdims=True) + EPS) * w


def _ffn(xn, wg_ref, wu_ref, wo_ref):
    acc = None
    for c in range(D_FF // FF_CHUNK):
        sl = slice(c * FF_CHUNK, (c + 1) * FF_CHUNK)
        g = _dot(xn, wg_ref[:, sl])
        u = _dot(xn, wu_ref[:, sl])
        a = (g * jax.nn.sigmoid(g) * u).astype(BF16)
        d = _dot(a, wo_ref[sl, :])
        acc = d if acc is None else acc + d
    return acc


def _const_spec(shape, grid_rank):
    zeros = (0,) * len(shape)
    if grid_rank == 1:
        index_map = lambda i: zeros
    elif grid_rank == 2:
        index_map = lambda i, j: zeros
    else:
        index_map = lambda i, j, k: zeros
    return pl.BlockSpec(shape, index_map, pipeline_mode=pl.Buffered(1))


def _row_spec(tm, width):
    return pl.BlockSpec((tm, width), lambda i: (i, 0))


def _rope_kernel(pos_ref, inv_ref, cmask_ref, sgn_ref, ck_ref, sk_ref):
    ang = pos_ref[...].astype(F32) * inv_ref[...]
    ck_ref[...] = jnp.cos(ang) * cmask_ref[...]
    sk_ref[...] = jnp.sin(ang) * sgn_ref[...]


def _rope_tables(positions):
    t = positions.size
    tm = min(t, 1024)
    inv = ROPE_THETA ** (-jnp.arange(0, QK_ROPE_DIM, 2, dtype=F32) / QK_ROPE_DIM)
    zeros_lo = jnp.zeros((ROPE_LO,), F32)
    zeros_hi = jnp.zeros((QK_PAD - ROPE_LO - QK_ROPE_DIM,), F32)
    ones_h = jnp.ones((ROPE_HALF,), F32)
    inv_row = jnp.concatenate([zeros_lo, inv, inv, zeros_hi])[None, :]
    cmask = jnp.concatenate([zeros_lo, ones_h, ones_h, zeros_hi])[None, :]
    sgn = jnp.concatenate([zeros_lo, -ones_h, ones_h, zeros_hi])[None, :]
    row = _const_spec((1, QK_PAD), 1)
    return pl.pallas_call(
        _rope_kernel,
        grid=(t // tm,),
        in_specs=[pl.BlockSpec((tm, 1), lambda i: (i, 0)), row, row, row],
        out_specs=[_row_spec(tm, QK_PAD), _row_spec(tm, QK_PAD)],
        out_shape=[jax.ShapeDtypeStruct((t, QK_PAD), F32)] * 2,
        compiler_params=pltpu.CompilerParams(dimension_semantics=("parallel",)),
        name="rope_tables",
    )(positions.reshape(t, 1), inv_row, cmask, sgn)


def _pre_kernel(x_ref, n1_ref, wg_ref, wu_ref, wo_ref, nm_ref, win_ref, qn_ref, kn_ref,
                wq1_ref, wq2_ref, wk_ref, wv_ref, ck_ref, sk_ref,
                h_ref, z_ref, xbc_ref, dt_ref, q_ref, k_ref, v_ref):
    x = x_ref[...]
    xn = _rms(x, n1_ref[...]).astype(BF16)
    h1 = x + 0.5 * _ffn(xn, wg_ref, wu_ref, wo_ref)
    h_ref[...] = h1
    u = _rms(h1, nm_ref[...]).astype(BF16)
    z_ref[...] = _dot(u, win_ref[:, _C_Z:_C_XBC])
    xbc_ref[...] = _dot(u, win_ref[:, _C_XBC:_C_DT])
    dt_ref[...] = _dot(u, win_ref[:, _C_DT:_C_QA])
    qa = _dot(u, win_ref[:, _C_QA:_C_KVA])
    kva = _dot(u, win_ref[:, _C_KVA:_C_KRA])
    kra = _dot(u, win_ref[:, _C_KRA:_C_KRB])
    krb = _dot(u, win_ref[:, _C_KRB:_C_END])
    qn = _rms(qa, qn_ref[...]).astype(BF16)
    kvn = _rms(kva, kn_ref[...]).astype(BF16)
    ck = ck_ref[...]
    sk = sk_ref[...]
    kr = kra * ck + krb * sk
    lane = lax.broadcasted_iota(jnp.int32, (1, QK_PAD), 1)
    cq = (ck + (lane < ROPE_LO).astype(F32)) * (SOFTMAX_SCALE * LOG2_E)
    sq = sk * (SOFTMAX_SCALE * LOG2_E)
    for h in range(MLA_HEADS):
        sl = slice(h * QK_PAD, (h + 1) * QK_PAD)
        q_ref[:, sl] = (_dot(qn, wq1_ref[:, sl]) * cq + _dot(qn, wq2_ref[:, sl]) * sq).astype(BF16)
        k_ref[:, sl] = (_dot(kvn, wk_ref[:, sl]) + kr).astype(BF16)
    v_ref[0] = lax.dot_general(wv_ref[...], kvn, (((1,), (1,)), ((), ())),
                               preferred_element_type=F32).astype(BF16)


def _pre_call(h, n1, wg, wu, wo, nm, win, qn, kn, wq1, wq2, wk, wv, ck, sk):
    t = h.shape[0]
    tm = ATTN_TK
    consts = [n1, wg, wu, wo, nm, win, qn, kn, wq1, wq2, wk, wv]
    in_specs = ([_row_spec(tm, D_MODEL)] + [_const_spec(c.shape, 1) for c in consts]
                + [_row_spec(tm, QK_PAD), _row_spec(tm, QK_PAD)])
    widths = [(D_MODEL, F32), (D_SSM, F32), (CONV_DIM, F32), (DT_PAD, F32),
              (MLA_HEADS * QK_PAD, BF16), (MLA_HEADS * QK_PAD, BF16)]
    return pl.pallas_call(
        _pre_kernel,
        grid=(t // tm,),
        in_specs=in_specs,
        out_specs=[_row_spec(tm, w) for w, _ in widths]
        + [pl.BlockSpec((1, D_MLA, tm), lambda i: (i, 0, 0))],
        out_shape=[jax.ShapeDtypeStruct((t, w), d) for w, d in widths]
        + [jax.ShapeDtypeStruct((t // tm, D_MLA, tm), BF16)],
        compiler_params=pltpu.CompilerParams(
            dimension_semantics=("parallel",), vmem_limit_bytes=V7X_VMEM_LIMIT_BYTES),
        name="pre",
    )(h, *consts, ck, sk)


def _split_dot(v, m, parts, left=False):
    out = None
    rem = v
    for _ in range(parts):
        piece = rem.astype(BF16)
        rem = rem - piece.astype(F32)
        d = _dot(m, piece) if left else _dot(piece, m)
        out = d if out is None else out + d
    return out


def _softplus(x):
    return jnp.maximum(x, 0.0) + jnp.log1p(jnp.exp(-jnp.abs(x)))


def _ssd_kernel(xbc_ref, z_ref, dt_ref, cw_ref, cb_ref, dtb_ref, alog_ref, dskip_ref, nw_ref,
                tri_ref, exp_ref, y_ref, state_ref, xpad_ref):
    q = SSD_BLOCK
    hp = SSD_HEADS_PER_GROUP * SSD_HEAD_DIM

    @pl.when(pl.program_id(1) == 0)
    def _():
        state_ref[...] = jnp.zeros_like(state_ref)
        xpad_ref[0:CONV_HIST, :] = jnp.zeros((CONV_HIST, CONV_DIM), F32)

    xr = xbc_ref[...]
    xpad_ref[CONV_HIST:CONV_HIST + q, :] = xr
    conv = cb_ref[...]
    for k in range(CONV_WIDTH):
        start = CONV_HIST - (CONV_WIDTH - 1) + k
        conv = conv + cw_ref[k:k + 1, :] * xpad_ref[start:start + q, :]
    xpad_ref[0:CONV_HIST, :] = xr[q - CONV_HIST:q, :]
    xbc = conv * jax.nn.sigmoid(conv)
    xs = xbc[:, :D_SSM]

    lane = lax.broadcasted_iota(jnp.int32, (1, DT_PAD), 1)
    head_lane = lane < SSD_HEADS
    dt = jnp.where(head_lane, _softplus(dt_ref[...] + dtb_ref[...]), 0.0)
    a_neg = jnp.where(head_lane, -jnp.exp(alog_ref[...]), 0.0)
    a_cum = _split_dot(dt * a_neg, tri_ref[...], 3, left=True)
    a_cum_t = a_cum.T
    a_last = a_cum[q - 1:q, :]
    to_end = jnp.exp(a_last - a_cum)
    from_start = jnp.exp(a_cum)

    expand = exp_ref[...]
    dt_e = _split_dot(dt, expand, 2)
    w_e = _split_dot(dt * to_end, expand, 2)
    fs_e = _split_dot(from_start, expand, 2)
    xdt = (xs * dt_e).astype(BF16)
    xw = (xs * w_e).astype(BF16)

    row = lax.broadcasted_iota(jnp.int32, (q, q), 0)
    col = lax.broadcasted_iota(jnp.int32, (q, q), 1)
    causal = col <= row
    lane_pair = lax.broadcasted_iota(jnp.int32, (1, 2 * SSD_HEAD_DIM), 1)
    first_of_pair = lane_pair < SSD_HEAD_DIM

    z = z_ref[...]
    for g in range(SSD_GROUPS):
        b_g = xbc[:, D_SSM + g * SSD_STATE:D_SSM + (g + 1) * SSD_STATE]
        c_g = xbc[:, D_SSM + (SSD_GROUPS + g) * SSD_STATE:D_SSM + (SSD_GROUPS + g + 1) * SSD_STATE]
        b_bf = b_g.astype(BF16)
        c_bf = c_g.astype(BF16)
        cb = lax.dot_general(c_bf, b_bf, (((1,), (1,)), ((), ())), preferred_element_type=F32)
        gsl = slice(g * hp, (g + 1) * hp)
        state = state_ref[g]
        y_g = _dot(c_bf, state.astype(BF16)) * fs_e[:, gsl] + xs[:, gsl] * dskip_ref[:, gsl]
        pieces = []
        for pair in range(SSD_HEADS_PER_GROUP // 2):
            outs = []
            for sub in range(2):
                h = g * SSD_HEADS_PER_GROUP + 2 * pair + sub
                seg = a_cum[:, h:h + 1] - a_cum_t[h:h + 1, :]
                m = (cb * jnp.exp(jnp.where(causal, seg, NEG_BIG))).astype(BF16)
                psl = slice(g * hp + pair * 2 * SSD_HEAD_DIM, g * hp + (pair + 1) * 2 * SSD_HEAD_DIM)
                outs.append(_dot(m, xdt[:, psl]))
            pieces.append(jnp.where(first_of_pair, outs[0], outs[1]))
        y_g = y_g + jnp.concatenate(pieces, axis=-1)
        y_g = y_g * (z[:, gsl] * jax.nn.sigmoid(z[:, gsl]))
        y_g = y_g * lax.rsqrt(jnp.mean(y_g * y_g, axis=-1, keepdims=True) + EPS) * nw_ref[:, gsl]
        y_ref[:, gsl] = y_g.astype(y_ref.dtype)
        state_ref[g] = state * fs_e[q - 1:q, gsl] + _dot(b_g.T.astype(BF16), xw[:, gsl])


def _ssd_call(xbc, z, dtm, cw, cb, dtb, alog, dskip, nw, tri, expand, batch, seq):
    q = SSD_BLOCK
    nblk = seq // q
    hp = SSD_HEADS_PER_GROUP * SSD_HEAD_DIM
    blk = lambda w: pl.BlockSpec((q, w), lambda b, c: (b * nblk + c, 0))
    consts = [cw, cb, dtb, alog, dskip, nw, tri, expand]
    return pl.pallas_call(
        _ssd_kernel,
        grid=(batch, nblk),
        in_specs=[blk(CONV_DIM), blk(D_SSM), blk(DT_PAD)] + [_const_spec(c.shape, 2) for c in consts],
        out_specs=blk(D_SSM),
        out_shape=jax.ShapeDtypeStruct((batch * seq, D_SSM), BF16),
        scratch_shapes=[pltpu.VMEM((SSD_GROUPS, SSD_STATE, hp), F32),
                        pltpu.VMEM((CONV_HIST + q, CONV_DIM), F32)],
        compiler_params=pltpu.CompilerParams(
            dimension_semantics=("arbitrary", "arbitrary"), vmem_limit_bytes=V7X_VMEM_LIMIT_BYTES),
        name="ssd",
    )(xbc, z, dtm, *consts)


def _attn_kernel(q_ref, k_ref, vt_ref, o_ref):
    tq, tk = q_ref.shape[0], ATTN_TK
    qi = pl.program_id(2)
    heads = range(ATTN_HEADS_PER_STEP)

    def step(j, carry, mask):
        start = pl.multiple_of(j * tk, tk)
        out = []
        scores = []
        for g in heads:
            k = k_ref[pl.ds(start, tk), g * QK_PAD:(g + 1) * QK_PAD]
            q = q_ref[:, g * QK_PAD:(g + 1) * QK_PAD]
            scores.append(lax.dot_general(k, q, (((1,), (1,)), ((), ())),
                                          preferred_element_type=F32))
        for g in heads:
            m, l, acc = carry[g]
            s = scores[g]
            if mask is not None:
                s = jnp.where(mask, s, NEG_BIG)
            m_new = jnp.maximum(m, jnp.max(s, axis=0, keepdims=True))
            alpha = jnp.exp2(m - m_new)
            p = jnp.exp2(s - m_new)
            l = alpha * l + jnp.sum(p, axis=0, keepdims=True)
            vt = vt_ref[j, g * V_HEAD_DIM:(g + 1) * V_HEAD_DIM, :]
            acc = alpha * acc + _dot(vt, p.astype(BF16))
            out.append((m_new, l, acc))
        return tuple(out)

    carry = tuple((jnp.full((1, tq), NEG_BIG, F32), jnp.zeros((1, tq), F32),
                   jnp.zeros((V_HEAD_DIM, tq), F32)) for _ in heads)
    key = lax.broadcasted_iota(jnp.int32, (tk, tq), 0)
    query_chunk = lax.broadcasted_iota(jnp.int32, (tk, tq), 1) // CHUNK
    nd = tq // tk
    for d in range(nd):
        mask = (key + d * tk) // CHUNK <= query_chunk
        carry = step(qi * nd + d, carry, mask)
    carry = lax.fori_loop(0, qi * nd, lambda j, c: step(j, c, None), carry)
    for g in heads:
        _, l, acc = carry[g]
        o_ref[:, g * V_HEAD_DIM:(g + 1) * V_HEAD_DIM] = (acc / l).T.astype(o_ref.dtype)


def _attn_call(q, k, vt, batch, seq):
    tq = min(seq, ATTN_TQ)
    nq = seq // tq
    g = ATTN_HEADS_PER_STEP
    return pl.pallas_call(
        _attn_kernel,
        grid=(batch, MLA_HEADS // g, nq),
        in_specs=[pl.BlockSpec((tq, g * QK_PAD), lambda b, h, i: (b * nq + i, h)),
                  pl.BlockSpec((seq, g * QK_PAD), lambda b, h, i: (b, h)),
                  pl.BlockSpec((seq // ATTN_TK, g * V_HEAD_DIM, ATTN_TK), lambda b, h, i: (b, h, 0))],
        out_specs=pl.BlockSpec((tq, g * V_HEAD_DIM), lambda b, h, i: (b * nq + i, h)),
        out_shape=jax.ShapeDtypeStruct((batch * seq, D_MLA), BF16),
        compiler_params=pltpu.CompilerParams(
            dimension_semantics=("parallel", "parallel", "arbitrary"),
            vmem_limit_bytes=V7X_VMEM_LIMIT_BYTES),
        name="attn",
    )(q, k, vt)


def _post_kernel(h_ref, ys_ref, ym_ref, p_ref, wos_ref, wom_ref, n2_ref, wg_ref, wu_ref, wo_ref,
                 np_ref, wgate_ref, wproj_ref, nf_ref, o_ref, *, final):
    h2 = h_ref[...] + _dot(ys_ref[...], wos_ref[...]) + _dot(ym_ref[...], wom_ref[...])
    xn = _rms(h2, n2_ref[...]).astype(BF16)
    h3 = h2 + 0.5 * _ffn(xn, wg_ref, wu_ref, wo_ref)
    gate = jax.nn.sigmoid(_dot(_rms(h3, np_ref[...]).astype(BF16), wgate_ref[...]))
    h4 = h3 + gate * _dot(p_ref[...].astype(BF16), wproj_ref[...])
    if final:
        h4 = _rms(h4, nf_ref[...])
    o_ref[...] = h4


def _post_call(h, ys, ym, p, wos, wom, n2, wg, wu, wo, npl, wgate, wproj, nf, final):
    t = h.shape[0]
    tm = min(t, TOKEN_TILE)
    consts = [wos, wom, n2, wg, wu, wo, npl, wgate, wproj, nf]
    in_specs = ([_row_spec(tm, D_MODEL), _row_spec(tm, D_SSM), _row_spec(tm, D_MLA),
                 _row_spec(tm, PLE_DIM)] + [_const_spec(c.shape, 1) for c in consts])
    return pl.pallas_call(
        functools.partial(_post_kernel, final=final),
        grid=(t // tm,),
        in_specs=in_specs,
        out_specs=_row_spec(tm, D_MODEL),
        out_shape=jax.ShapeDtypeStruct((t, D_MODEL), F32),
        compiler_params=pltpu.CompilerParams(
            dimension_semantics=("parallel",), vmem_limit_bytes=V7X_VMEM_LIMIT_BYTES),
        name="post",
    )(h, ys, ym, p, *consts)


def _pad_cols(w, lo, width):
    return jnp.pad(w, ((0, 0), (lo, width - lo - w.shape[1])))


def _swap_halves(w):
    return jnp.concatenate([w[:, ROPE_HALF:], w[:, :ROPE_HALF]], axis=1)


def _layout_in_proj(w):
    o = 0
    w_z = w[:, o:o + D_SSM]; o += D_SSM
    w_xbc = w[:, o:o + CONV_DIM]; o += CONV_DIM
    w_dt = w[:, o:o + SSD_HEADS]; o += SSD_HEADS
    w_qa = w[:, o:o + Q_LORA_RANK]; o += Q_LORA_RANK
    w_kva = w[:, o:o + KV_LORA_RANK]; o += KV_LORA_RANK
    w_kr = w[:, o:o + QK_ROPE_DIM]
    return jnp.concatenate([
        w_z, w_xbc, _pad_cols(w_dt, 0, DT_PAD), w_qa, w_kva,
        _pad_cols(w_kr, ROPE_LO, QK_PAD), _pad_cols(_swap_halves(w_kr), ROPE_LO, QK_PAD)],
        axis=1).astype(BF16)


def _layout_q_b(w):
    w = w.reshape(Q_LORA_RANK, MLA_HEADS, QK_NOPE_DIM + QK_ROPE_DIM)
    plain, swapped = [], []
    for h in range(MLA_HEADS):
        wh = w[:, h, :]
        plain.append(_pad_cols(wh, 0, QK_PAD))
        swapped.append(_pad_cols(_swap_halves(wh[:, QK_NOPE_DIM:]), ROPE_LO, QK_PAD))
    return (jnp.concatenate(plain, axis=1).astype(BF16),
            jnp.concatenate(swapped, axis=1).astype(BF16))


def _layout_kv_b(w):
    w = w.reshape(KV_LORA_RANK, MLA_HEADS, QK_NOPE_DIM + V_HEAD_DIM)
    wk = jnp.concatenate([_pad_cols(w[:, h, :QK_NOPE_DIM], 0, QK_PAD) for h in range(MLA_HEADS)], axis=1)
    wv_t = w[:, :, QK_NOPE_DIM:].reshape(KV_LORA_RANK, D_MLA).T
    return wk.astype(BF16), wv_t.astype(BF16)


def _row(v):
    return v.reshape(1, -1).astype(F32)


def kernel(x, p, positions, ffn1_norm, ffn1_w_in, ffn1_w_out, mix_norm, w_in_mix, conv_w, conv_b,
           dt_bias, a_log, d_skip, ssd_norm, q_a_norm, w_q_b, kv_a_norm, w_kv_b, w_out_mix,
           ffn2_norm, ffn2_w_in, ffn2_w_out, ple_norm, w_ple_gate, w_ple_proj, final_norm):
    batch, seq, _ = x.shape
    depth = p.shape[0]
    t = batch * seq
    assert seq % SSD_BLOCK == 0 and seq % min(seq, ATTN_TQ) == 0 and t % TOKEN_TILE == 0

    ck, sk = _rope_tables(positions)
    q_idx = jnp.arange(SSD_BLOCK)
    tri = (q_idx[None, :] <= q_idx[:, None]).astype(BF16)
    expand = (jnp.arange(D_SSM)[None, :] // SSD_HEAD_DIM == jnp.arange(DT_PAD)[:, None]).astype(BF16)

    h = x.reshape(t, D_MODEL)
    for i in range(depth):
        wq1, wq2 = _layout_q_b(w_q_b[i])
        wk, wv = _layout_kv_b(w_kv_b[i])
        h, z, xbc, dtm, q, k, v = _pre_call(
            h, _row(ffn1_norm[i]),
            ffn1_w_in[i][:, :D_FF].astype(BF16), ffn1_w_in[i][:, D_FF:].astype(BF16),
            ffn1_w_out[i].astype(BF16), _row(mix_norm[i]), _layout_in_proj(w_in_mix[i]),
            _row(q_a_norm[i]), _row(kv_a_norm[i]), wq1, wq2, wk, wv, ck, sk)
        y_ssd = _ssd_call(
            xbc, z, dtm, conv_w[i].astype(F32), _row(conv_b[i]),
            _pad_cols(_row(dt_bias[i]), 0, DT_PAD), _pad_cols(_row(a_log[i]), 0, DT_PAD),
            _row(jnp.repeat(d_skip[i], SSD_HEAD_DIM)), _row(ssd_norm[i]), tri, expand, batch, seq)
        y_mla = _attn_call(q, k, v, batch, seq)
        h = _post_call(
            h, y_ssd, y_mla, p[i].reshape(t, PLE_DIM),
            w_out_mix[i][:D_SSM].astype(BF16), w_out_mix[i][D_SSM:].astype(BF16),
            _row(ffn2_norm[i]),
            ffn2_w_in[i][:, :D_FF].astype(BF16), ffn2_w_in[i][:, D_FF:].astype(BF16),
            ffn2_w_out[i].astype(BF16), _row(ple_norm[i]), w_ple_gate[i].astype(BF16),
            w_ple_proj[i].astype(BF16), _row(final_norm), final=(i == depth - 1))
    return h.reshape(batch, seq, D_MODEL)
```

```python
import functools

import jax
import jax.numpy as jnp
from jax import lax
from jax.experimental import pallas as pl
from jax.experimental.pallas import tpu as pltpu

F32 = jnp.float32
BF16 = jnp.bfloat16

D_MODEL = 1024
CHUNK = 64
PLE_DIM = 256
D_FF = 2816
EPS = 1e-6
SSD_HEADS = 16
SSD_HEAD_DIM = 64
D_SSM = SSD_HEADS * SSD_HEAD_DIM
SSD_GROUPS = 2
SSD_HEADS_PER_GROUP = SSD_HEADS // SSD_GROUPS
SSD_STATE = 128
CONV_WIDTH = 4
CONV_DIM = D_SSM + 2 * SSD_GROUPS * SSD_STATE
MLA_HEADS = 8
QK_NOPE_DIM = 128
QK_ROPE_DIM = 64
V_HEAD_DIM = 128
Q_LORA_RANK = 384
KV_LORA_RANK = 256
D_MLA = MLA_HEADS * V_HEAD_DIM
ROPE_THETA = 10000.0
SOFTMAX_SCALE = (QK_NOPE_DIM + QK_ROPE_DIM) ** -0.5
LOG2_E = 1.4426950408889634

V7X_LANES = 128
V7X_SUBLANES = 8
V7X_MXU_DIM = 256
V7X_VMEM_LIMIT_BYTES = 56 * 1024 * 1024

QK_PAD = V7X_MXU_DIM
ROPE_HALF = QK_ROPE_DIM // 2
ROPE_TILE = V7X_LANES
DT_PAD = V7X_LANES
KR_PAD = V7X_LANES
FF_CHUNK = V7X_MXU_DIM
SSD_BLOCK = 256
CONV_HIST = V7X_SUBLANES
TOKEN_TILE = 512
ATTN_TQ = 512
ATTN_TK = 256
ATTN_HEADS_PER_STEP = 4
ATTN_LOOP_TILES = 2
NEG_BIG = -1e30

_C_Z = 0
_C_XBC = _C_Z + D_SSM
_C_QA = _C_XBC + CONV_DIM
_C_KVA = _C_QA + Q_LORA_RANK
_C_DT = _C_KVA + KV_LORA_RANK
_C_KR = _C_DT + DT_PAD
_C_END = _C_KR + KR_PAD


def _dot(a, b):
    return jnp.dot(a, b, preferred_element_type=F32)


def _rms(x, w):
    return x * lax.rsqrt(jnp.mean(x * x, axis=-1, keepdims=True) + EPS) * w


def _ffn(xn, wg_ref, wu_ref, wo_ref):
    n = D_FF // FF_CHUNK
    cols = lambda c: slice(c * FF_CHUNK, (c + 1) * FF_CHUNK)
    up = lambda c: (_dot(xn, wg_ref[:, cols(c)]), _dot(xn, wu_ref[:, cols(c)]))
    acc = None
    gu = up(0)
    for c in range(n):
        gu_next = up(c + 1) if c + 1 < n else None
        g, u = gu
        a = (g * jax.nn.sigmoid(g) * u).astype(BF16)
        d = _dot(a, wo_ref[cols(c), :])
        acc = d if acc is None else acc + d
        gu = gu_next
    return acc


def _const_spec(shape, grid_rank):
    zeros = (0,) * len(shape)
    if grid_rank == 1:
        index_map = lambda i: zeros
    elif grid_rank == 2:
        index_map = lambda i, j: zeros
    else:
        index_map = lambda i, j, k: zeros
    return pl.BlockSpec(shape, index_map, pipeline_mode=pl.Buffered(1))


def _row_spec(tm, width):
    return pl.BlockSpec((tm, width), lambda i: (i, 0))


def _rope_kernel(pos_ref, inv_ref, lo_ref, hi_ref, c_ref, slo_ref, shi_ref):
    ang = pos_ref[...].astype(F32) * inv_ref[...]
    sin = jnp.sin(ang)
    c_ref[...] = jnp.cos(ang)
    slo_ref[...] = sin * lo_ref[...]
    shi_ref[...] = sin * hi_ref[...]


def _rope_tables(positions):
    t = positions.size
    tm = min(t, 1024)
    groups = ROPE_TILE // QK_ROPE_DIM
    inv = ROPE_THETA ** (-jnp.arange(0, QK_ROPE_DIM, 2, dtype=F32) / QK_ROPE_DIM)
    ones_h = jnp.ones((ROPE_HALF,), F32)
    zeros_h = jnp.zeros((ROPE_HALF,), F32)
    inv_row = jnp.tile(inv, 2 * groups)[None, :]
    lo = jnp.tile(jnp.concatenate([-ones_h, zeros_h]), groups)[None, :]
    hi = jnp.tile(jnp.concatenate([zeros_h, ones_h]), groups)[None, :]
    row = _const_spec((1, ROPE_TILE), 1)
    return pl.pallas_call(
        _rope_kernel,
        grid=(t // tm,),
        in_specs=[pl.BlockSpec((tm, 1), lambda i: (i, 0)), row, row, row],
        out_specs=[_row_spec(tm, ROPE_TILE)] * 3,
        out_shape=[jax.ShapeDtypeStruct((t, ROPE_TILE), F32)] * 3,
        compiler_params=pltpu.CompilerParams(dimension_semantics=("parallel",)),
        name="rope_tables",
    )(positions.reshape(t, 1), inv_row, lo, hi)


def _rotate(x, c, s_lo, s_hi):
    ahead = pltpu.roll(x, shift=ROPE_TILE - ROPE_HALF, axis=1)
    behind = pltpu.roll(x, shift=ROPE_HALF, axis=1)
    return x * c + ahead * s_lo + behind * s_hi


def _pre_kernel(x_ref, n1_ref, wg_ref, wu_ref, wo_ref, nm_ref, win_ref, qn_ref, kn_ref,
                wqn_ref, wqr_ref, wk_ref, wv_ref, c_ref, slo_ref, shi_ref,
                h_ref, z_ref, xbc_ref, dt_ref, q_ref, k_ref, v_ref):
    x = x_ref[...]
    xn = _rms(x, n1_ref[...]).astype(BF16)
    h1 = x + 0.5 * _ffn(xn, wg_ref, wu_ref, wo_ref)
    h_ref[...] = h1
    u = _rms(h1, nm_ref[...]).astype(BF16)
    z_ref[...] = _dot(u, win_ref[:, _C_Z:_C_XBC])
    xbc_ref[...] = _dot(u, win_ref[:, _C_XBC:_C_QA])
    tail = _dot(u, win_ref[:, _C_QA:_C_END])
    dt_ref[...] = tail[:, _C_DT - _C_QA:_C_KR - _C_QA]
    qn = _rms(tail[:, :_C_KVA - _C_QA], qn_ref[...]).astype(BF16)
    kvn = _rms(tail[:, _C_KVA - _C_QA:_C_DT - _C_QA], kn_ref[...]).astype(BF16)
    c, s_lo, s_hi = c_ref[...], slo_ref[...], shi_ref[...]
    lane = lax.broadcasted_iota(jnp.int32, (1, ROPE_TILE), 1)
    first_group = lane < QK_ROPE_DIM
    kr = _rotate(tail[:, _C_KR - _C_QA:], c, s_lo, s_hi).astype(BF16)
    q_scale = SOFTMAX_SCALE * LOG2_E
    q_nope = _dot(qn, wqn_ref[...]) * q_scale
    k_nope = _dot(kvn, wk_ref[...])
    q_rope = _dot(qn, wqr_ref[...])
    for pair in range(MLA_HEADS // 2):
        tile = slice(pair * ROPE_TILE, (pair + 1) * ROPE_TILE)
        rot = _rotate(q_rope[:, tile], c, s_lo, s_hi) * q_scale
        for sub, part in enumerate((jnp.where(first_group, rot, 0.0),
                                    jnp.where(first_group, pltpu.roll(rot, shift=QK_ROPE_DIM, axis=1), 0.0))):
            base = (2 * pair + sub) * QK_PAD
            q_ref[:, base + QK_NOPE_DIM:base + QK_PAD] = part.astype(BF16)
    for h in range(MLA_HEADS):
        nope = slice(h * QK_NOPE_DIM, (h + 1) * QK_NOPE_DIM)
        q_ref[:, h * QK_PAD:h * QK_PAD + QK_NOPE_DIM] = q_nope[:, nope].astype(BF16)
        k_ref[:, h * QK_PAD:h * QK_PAD + QK_NOPE_DIM] = k_nope[:, nope].astype(BF16)
        k_ref[:, h * QK_PAD + QK_NOPE_DIM:(h + 1) * QK_PAD] = kr
    v_ref[0] = lax.dot_general(wv_ref[...], kvn, (((1,), (1,)), ((), ())),
                               preferred_element_type=F32).astype(BF16)


def _pre_call(h, n1, wg, wu, wo, nm, win, qn, kn, wqn, wqr, wk, wv, rope):
    t = h.shape[0]
    tm = ATTN_TK
    consts = [n1, wg, wu, wo, nm, win, qn, kn, wqn, wqr, wk, wv]
    in_specs = ([_row_spec(tm, D_MODEL)] + [_const_spec(c.shape, 1) for c in consts]
                + [_row_spec(tm, ROPE_TILE)] * len(rope))
    widths = [(D_MODEL, F32), (D_SSM, F32), (CONV_DIM, F32), (DT_PAD, F32),
              (MLA_HEADS * QK_PAD, BF16), (MLA_HEADS * QK_PAD, BF16)]
    return pl.pallas_call(
        _pre_kernel,
        grid=(t // tm,),
        in_specs=in_specs,
        out_specs=[_row_spec(tm, w) for w, _ in widths]
        + [pl.BlockSpec((1, D_MLA, tm), lambda i: (i, 0, 0))],
        out_shape=[jax.ShapeDtypeStruct((t, w), d) for w, d in widths]
        + [jax.ShapeDtypeStruct((t // tm, D_MLA, tm), BF16)],
        compiler_params=pltpu.CompilerParams(
            dimension_semantics=("parallel",), vmem_limit_bytes=V7X_VMEM_LIMIT_BYTES),
        name="pre",
    )(h, *consts, *rope)


def _split_dot(v, m, parts, left=False):
    out = None
    rem = v
    for _ in range(parts):
        piece = rem.astype(BF16)
        rem = rem - piece.astype(F32)
        d = _dot(m, piece) if left else _dot(piece, m)
        out = d if out is None else out + d
    return out


def _softplus(x):
    return jnp.maximum(x, 0.0) + jnp.log1p(jnp.exp(-jnp.abs(x)))


def _ssd_kernel(xbc_ref, z_ref, dt_ref, cw_ref, cb_ref, dtb_ref, alog_ref, dskip_ref, nw_ref,
                tri_ref, exp_ref, y_ref, state_ref, hist_ref):
    q = SSD_BLOCK
    hp = SSD_HEADS_PER_GROUP * SSD_HEAD_DIM

    @pl.when(pl.program_id(1) == 0)
    def _():
        state_ref[...] = jnp.zeros_like(state_ref)
        hist_ref[...] = jnp.zeros_like(hist_ref)

    xr = xbc_ref[...]
    xpad = jnp.concatenate([hist_ref[...], xr], axis=0)
    conv = cb_ref[...] + cw_ref[CONV_WIDTH - 1:CONV_WIDTH, :] * xr
    for back in range(1, CONV_WIDTH):
        shifted = pltpu.roll(xpad, shift=back, axis=0)[CONV_HIST:, :]
        conv = conv + cw_ref[CONV_WIDTH - 1 - back:CONV_WIDTH - back, :] * shifted
    hist_ref[...] = xr[q - CONV_HIST:q, :]
    xbc = conv * jax.nn.sigmoid(conv)
    xs = xbc[:, :D_SSM]

    lane = lax.broadcasted_iota(jnp.int32, (1, DT_PAD), 1)
    head_lane = lane < SSD_HEADS
    dt = jnp.where(head_lane, _softplus(dt_ref[...] + dtb_ref[...]), 0.0)
    a_neg = jnp.where(head_lane, -jnp.exp(alog_ref[...]), 0.0)
    a_cum = _split_dot(dt * a_neg, tri_ref[...], 3, left=True)
    a_cum_t = a_cum.T
    a_last = a_cum[q - 1:q, :]
    to_end = jnp.exp(a_last - a_cum)
    from_start = jnp.exp(a_cum)

    expand = exp_ref[...]
    dt_e = _split_dot(dt, expand, 2)
    w_e = _split_dot(dt * to_end, expand, 2)
    fs_e = _split_dot(from_start, expand, 2)
    xdt = (xs * dt_e).astype(BF16)
    xw = (xs * w_e).astype(BF16)

    row = lax.broadcasted_iota(jnp.int32, (q, q), 0)
    col = lax.broadcasted_iota(jnp.int32, (q, q), 1)
    causal = col <= row
    lane_pair = lax.broadcasted_iota(jnp.int32, (1, 2 * SSD_HEAD_DIM), 1)
    first_of_pair = lane_pair < SSD_HEAD_DIM

    z = z_ref[...]
    for g in range(SSD_GROUPS):
        b_g = xbc[:, D_SSM + g * SSD_STATE:D_SSM + (g + 1) * SSD_STATE]
        c_g = xbc[:, D_SSM + (SSD_GROUPS + g) * SSD_STATE:D_SSM + (SSD_GROUPS + g + 1) * SSD_STATE]
        b_bf = b_g.astype(BF16)
        c_bf = c_g.astype(BF16)
        cb = lax.dot_general(c_bf, b_bf, (((1,), (1,)), ((), ())), preferred_element_type=F32)
        gsl = slice(g * hp, (g + 1) * hp)
        state = state_ref[g]
        y_g = _dot(c_bf, state.astype(BF16)) * fs_e[:, gsl] + xs[:, gsl] * dskip_ref[:, gsl]
        pieces = []
        for pair in range(SSD_HEADS_PER_GROUP // 2):
            outs = []
            for sub in range(2):
                h = g * SSD_HEADS_PER_GROUP + 2 * pair + sub
                seg = a_cum[:, h:h + 1] - a_cum_t[h:h + 1, :]
                m = (cb * jnp.exp(jnp.where(causal, seg, NEG_BIG))).astype(BF16)
                psl = slice(g * hp + pair * 2 * SSD_HEAD_DIM, g * hp + (pair + 1) * 2 * SSD_HEAD_DIM)
                outs.append(_dot(m, xdt[:, psl]))
            pieces.append(jnp.where(first_of_pair, outs[0], outs[1]))
        y_g = y_g + jnp.concatenate(pieces, axis=-1)
        y_g = y_g * (z[:, gsl] * jax.nn.sigmoid(z[:, gsl]))
        y_g = y_g * lax.rsqrt(jnp.mean(y_g * y_g, axis=-1, keepdims=True) + EPS) * nw_ref[:, gsl]
        y_ref[:, gsl] = y_g.astype(y_ref.dtype)
        state_ref[g] = state * fs_e[q - 1:q, gsl] + _dot(b_g.T.astype(BF16), xw[:, gsl])


def _ssd_call(xbc, z, dtm, cw, cb, dtb, alog, dskip, nw, tri, expand, batch, seq):
    q = SSD_BLOCK
    nblk = seq // q
    hp = SSD_HEADS_PER_GROUP * SSD_HEAD_DIM
    blk = lambda w: pl.BlockSpec((q, w), lambda b, c: (b * nblk + c, 0))
    consts = [cw, cb, dtb, alog, dskip, nw, tri, expand]
    return pl.pallas_call(
        _ssd_kernel,
        grid=(batch, nblk),
        in_specs=[blk(CONV_DIM), blk(D_SSM), blk(DT_PAD)] + [_const_spec(c.shape, 2) for c in consts],
        out_specs=blk(D_SSM),
        out_shape=jax.ShapeDtypeStruct((batch * seq, D_SSM), BF16),
        scratch_shapes=[pltpu.VMEM((SSD_GROUPS, SSD_STATE, hp), F32),
                        pltpu.VMEM((CONV_HIST, CONV_DIM), F32)],
        compiler_params=pltpu.CompilerParams(
            dimension_semantics=("arbitrary", "arbitrary"), vmem_limit_bytes=V7X_VMEM_LIMIT_BYTES),
        name="ssd",
    )(xbc, z, dtm, *consts)


def _attn_kernel(q_ref, k_ref, vt_ref, o_ref, m_ref, l_ref, acc_ref):
    tq, tk = q_ref.shape[0], ATTN_TK
    qi = pl.program_id(2)
    heads = range(ATTN_HEADS_PER_STEP)

    def block(j, tiles, q_lo, mask, first):
        start = pl.multiple_of(j * tk, tk)
        scores = []
        for g in heads:
            k = k_ref[pl.ds(start, tiles * tk), g * QK_PAD:(g + 1) * QK_PAD]
            q = q_ref[q_lo:, g * QK_PAD:(g + 1) * QK_PAD]
            scores.append(lax.dot_general(k, q, (((1,), (1,)), ((), ())),
                                          preferred_element_type=F32))
        for g in heads:
            s = scores[g]
            if mask is not None:
                s = jnp.where(mask, s, NEG_BIG)
            m_blk = jnp.max(s, axis=0, keepdims=True)
            if first:
                m_new = m_blk
            else:
                m_old = m_ref[g, :, q_lo:]
                m_new = jnp.maximum(m_old, m_blk)
                alpha = jnp.exp2(m_old - m_new)
            p = jnp.exp2(s - m_new)
            p_sum = jnp.sum(p, axis=0, keepdims=True)
            p = p.astype(BF16)
            pv = None
            for i in range(tiles):
                d = _dot(vt_ref[j + i, g * V_HEAD_DIM:(g + 1) * V_HEAD_DIM, :], p[i * tk:(i + 1) * tk, :])
                pv = d if pv is None else pv + d
            if first:
                l_ref[g, :, q_lo:] = p_sum
                acc_ref[g, :, q_lo:] = pv
            else:
                l_ref[g, :, q_lo:] = alpha * l_ref[g, :, q_lo:] + p_sum
                acc_ref[g, :, q_lo:] = alpha * acc_ref[g, :, q_lo:] + pv
            m_ref[g, :, q_lo:] = m_new

    for d in range(tq // tk):
        q_lo = d * tk
        key = lax.broadcasted_iota(jnp.int32, (tk, tq - q_lo), 0)
        query = lax.broadcasted_iota(jnp.int32, (tk, tq - q_lo), 1)
        block(qi * (tq // tk) + d, 1, q_lo, key // CHUNK <= query // CHUNK, first=(d == 0))

    @pl.loop(0, qi * (tq // tk) // ATTN_LOOP_TILES)
    def _(j):
        block(j * ATTN_LOOP_TILES, ATTN_LOOP_TILES, 0, None, first=False)

    for g in heads:
        o_ref[:, g * V_HEAD_DIM:(g + 1) * V_HEAD_DIM] = (acc_ref[g] / l_ref[g]).T.astype(o_ref.dtype)


def _attn_call(q, k, vt, batch, seq):
    tq = min(seq, ATTN_TQ)
    nq = seq // tq
    g = ATTN_HEADS_PER_STEP
    return pl.pallas_call(
        _attn_kernel,
        grid=(batch, MLA_HEADS // g, nq),
        in_specs=[pl.BlockSpec((tq, g * QK_PAD), lambda b, h, i: (b * nq + i, h)),
                  pl.BlockSpec((seq, g * QK_PAD), lambda b, h, i: (b, h)),
                  pl.BlockSpec((seq // ATTN_TK, g * V_HEAD_DIM, ATTN_TK), lambda b, h, i: (b, h, 0))],
        out_specs=pl.BlockSpec((tq, g * V_HEAD_DIM), lambda b, h, i: (b * nq + i, h)),
        out_shape=jax.ShapeDtypeStruct((batch * seq, D_MLA), BF16),
        scratch_shapes=[pltpu.VMEM((g, 1, tq), F32), pltpu.VMEM((g, 1, tq), F32),
                        pltpu.VMEM((g, V_HEAD_DIM, tq), F32)],
        compiler_params=pltpu.CompilerParams(
            dimension_semantics=("parallel", "parallel", "arbitrary"),
            vmem_limit_bytes=V7X_VMEM_LIMIT_BYTES),
        name="attn",
    )(q, k, vt)


def _post_kernel(h_ref, ys_ref, ym_ref, p_ref, wos_ref, wom_ref, n2_ref, wg_ref, wu_ref, wo_ref,
                 np_ref, wgate_ref, wproj_ref, nf_ref, o_ref, *, final):
    h2 = h_ref[...] + _dot(ys_ref[...], wos_ref[...]) + _dot(ym_ref[...], wom_ref[...])
    xn = _rms(h2, n2_ref[...]).astype(BF16)
    h3 = h2 + 0.5 * _ffn(xn, wg_ref, wu_ref, wo_ref)
    gate = jax.nn.sigmoid(_dot(_rms(h3, np_ref[...]).astype(BF16), wgate_ref[...]))
    h4 = h3 + gate * _dot(p_ref[...].astype(BF16), wproj_ref[...])
    if final:
        h4 = _rms(h4, nf_ref[...])
    o_ref[...] = h4


def _post_call(h, ys, ym, p, wos, wom, n2, wg, wu, wo, npl, wgate, wproj, nf, final):
    t = h.shape[0]
    tm = min(t, TOKEN_TILE)
    consts = [wos, wom, n2, wg, wu, wo, npl, wgate, wproj, nf]
    in_specs = ([_row_spec(tm, D_MODEL), _row_spec(tm, D_SSM), _row_spec(tm, D_MLA),
                 _row_spec(tm, PLE_DIM)] + [_const_spec(c.shape, 1) for c in consts])
    return pl.pallas_call(
        functools.partial(_post_kernel, final=final),
        grid=(t // tm,),
        in_specs=in_specs,
        out_specs=_row_spec(tm, D_MODEL),
        out_shape=jax.ShapeDtypeStruct((t, D_MODEL), F32),
        compiler_params=pltpu.CompilerParams(
            dimension_semantics=("parallel",), vmem_limit_bytes=V7X_VMEM_LIMIT_BYTES),
        name="post",
    )(h, ys, ym, p, *consts)


def _pad_cols(w, lo, width):
    return jnp.pad(w, ((0, 0), (lo, width - lo - w.shape[1])))


def _layout_in_proj(w):
    o = 0
    w_z = w[:, o:o + D_SSM]; o += D_SSM
    w_xbc = w[:, o:o + CONV_DIM]; o += CONV_DIM
    w_dt = w[:, o:o + SSD_HEADS]; o += SSD_HEADS
    w_qa = w[:, o:o + Q_LORA_RANK]; o += Q_LORA_RANK
    w_kva = w[:, o:o + KV_LORA_RANK]; o += KV_LORA_RANK
    w_kr = w[:, o:o + QK_ROPE_DIM]
    return jnp.concatenate([
        w_z, w_xbc, w_qa, w_kva, _pad_cols(w_dt, 0, DT_PAD), _pad_cols(w_kr, 0, KR_PAD)],
        axis=1).astype(BF16)


def _layout_q_b(w):
    w = w.reshape(Q_LORA_RANK, MLA_HEADS, QK_NOPE_DIM + QK_ROPE_DIM)
    return (w[:, :, :QK_NOPE_DIM].reshape(Q_LORA_RANK, MLA_HEADS * QK_NOPE_DIM).astype(BF16),
            w[:, :, QK_NOPE_DIM:].reshape(Q_LORA_RANK, MLA_HEADS * QK_ROPE_DIM).astype(BF16))


def _layout_kv_b(w):
    w = w.reshape(KV_LORA_RANK, MLA_HEADS, QK_NOPE_DIM + V_HEAD_DIM)
    wk = w[:, :, :QK_NOPE_DIM].reshape(KV_LORA_RANK, MLA_HEADS * QK_NOPE_DIM)
    wv_t = w[:, :, QK_NOPE_DIM:].reshape(KV_LORA_RANK, D_MLA).T
    return wk.astype(BF16), wv_t.astype(BF16)


def _row(v):
    return v.reshape(1, -1).astype(F32)


def kernel(x, p, positions, ffn1_norm, ffn1_w_in, ffn1_w_out, mix_norm, w_in_mix, conv_w, conv_b,
           dt_bias, a_log, d_skip, ssd_norm, q_a_norm, w_q_b, kv_a_norm, w_kv_b, w_out_mix,
           ffn2_norm, ffn2_w_in, ffn2_w_out, ple_norm, w_ple_gate, w_ple_proj, final_norm):
    batch, seq, _ = x.shape
    depth = p.shape[0]
    t = batch * seq
    assert seq % SSD_BLOCK == 0 and seq % min(seq, ATTN_TQ) == 0 and t % TOKEN_TILE == 0

    rope = _rope_tables(positions)
    q_idx = jnp.arange(SSD_BLOCK)
    tri = (q_idx[None, :] <= q_idx[:, None]).astype(BF16)
    expand = (jnp.arange(D_SSM)[None, :] // SSD_HEAD_DIM == jnp.arange(DT_PAD)[:, None]).astype(BF16)

    h = x.reshape(t, D_MODEL)
    for i in range(depth):
        wqn, wqr = _layout_q_b(w_q_b[i])
        wk, wv = _layout_kv_b(w_kv_b[i])
        h, z, xbc, dtm, q, k, v = _pre_call(
            h, _row(ffn1_norm[i]),
            ffn1_w_in[i][:, :D_FF].astype(BF16), ffn1_w_in[i][:, D_FF:].astype(BF16),
            ffn1_w_out[i].astype(BF16), _row(mix_norm[i]), _layout_in_proj(w_in_mix[i]),
            _row(q_a_norm[i]), _row(kv_a_norm[i]), wqn, wqr, wk, wv, rope)
        y_ssd = _ssd_call(
            xbc, z, dtm, conv_w[i].astype(F32), _row(conv_b[i]),
            _pad_cols(_row(dt_bias[i]), 0, DT_PAD), _pad_cols(_row(a_log[i]), 0, DT_PAD),
            _row(jnp.repeat(d_skip[i], SSD_HEAD_DIM)), _row(ssd_norm[i]), tri, expand, batch, seq)
        y_mla = _attn_call(q, k, v, batch, seq)
        h = _post_call(
            h, y_ssd, y_mla, p[i].reshape(t, PLE_DIM),
            w_out_mix[i][:D_SSM].astype(BF16), w_out_mix[i][D_SSM:].astype(BF16),
            _row(ffn2_norm[i]),
            ffn2_w_in[i][:, :D_FF].astype(BF16), ffn2_w_in[i][:, D_FF:].astype(BF16),
            ffn2_w_out[i].astype(BF16), _row(ple_norm[i]), w_ple_gate[i].astype(BF16),
            w_ple_proj[i].astype(BF16), _row(final_norm), final=(i == depth - 1))
    return h.reshape(batch, seq, D_MODEL)
```

```python
import functools

import jax
import jax.numpy as jnp
from jax import lax
from jax.experimental import pallas as pl
from jax.experimental.pallas import tpu as pltpu

F32 = jnp.float32
BF16 = jnp.bfloat16

D_MODEL = 1024
CHUNK = 64
PLE_DIM = 256
D_FF = 2816
EPS = 1e-6
SSD_HEADS = 16
SSD_HEAD_DIM = 64
D_SSM = SSD_HEADS * SSD_HEAD_DIM
SSD_GROUPS = 2
SSD_HEADS_PER_GROUP = SSD_HEADS // SSD_GROUPS
SSD_STATE = 128
CONV_WIDTH = 4
CONV_DIM = D_SSM + 2 * SSD_GROUPS * SSD_STATE
MLA_HEADS = 8
QK_NOPE_DIM = 128
QK_ROPE_DIM = 64
V_HEAD_DIM = 128
Q_LORA_RANK = 384
KV_LORA_RANK = 256
D_MLA = MLA_HEADS * V_HEAD_DIM
ROPE_THETA = 10000.0
SOFTMAX_SCALE = (QK_NOPE_DIM + QK_ROPE_DIM) ** -0.5
LOG2_E = 1.4426950408889634

V7X_LANES = 128
V7X_SUBLANES = 8
V7X_MXU_DIM = 256
V7X_VMEM_LIMIT_BYTES = 56 * 1024 * 1024

QK_PAD = V7X_MXU_DIM
ROPE_HALF = QK_ROPE_DIM // 2
ROPE_TILE = V7X_LANES
DT_PAD = V7X_LANES
KR_PAD = V7X_LANES
FF_CHUNK = V7X_MXU_DIM
SSD_BLOCK = 256
CONV_HIST = V7X_SUBLANES
TOKEN_TILE = 512
ATTN_TQ = 512
ATTN_TK = 256
ATTN_HEADS_PER_STEP = 4
ATTN_LOOP_TILES = 2
ATTN_ONES_ROWS = 2 * V7X_SUBLANES
NEG_BIG = -1e30

_C_Z = 0
_C_XBC = _C_Z + D_SSM
_C_QA = _C_XBC + CONV_DIM
_C_KVA = _C_QA + Q_LORA_RANK
_C_DT = _C_KVA + KV_LORA_RANK
_C_KR = _C_DT + DT_PAD
_C_END = _C_KR + KR_PAD


def _dot(a, b):
    return jnp.dot(a, b, preferred_element_type=F32)


def _rms(x, w):
    return x * lax.rsqrt(jnp.mean(x * x, axis=-1, keepdims=True) + EPS) * w


def _ffn(xn, wi_ref, wo_ref):
    n = D_FF // FF_CHUNK
    cols = lambda c: slice(c * FF_CHUNK, (c + 1) * FF_CHUNK)
    up_cols = lambda c: slice(D_FF + c * FF_CHUNK, D_FF + (c + 1) * FF_CHUNK)
    up = lambda c: (_dot(xn, wi_ref[:, cols(c)]), _dot(xn, wi_ref[:, up_cols(c)]))
    acc = None
    gu = up(0)
    for c in range(n):
        gu_next = up(c + 1) if c + 1 < n else None
        g, u = gu
        a = (g * jax.nn.sigmoid(g) * u).astype(BF16)
        d = _dot(a, wo_ref[cols(c), :])
        acc = d if acc is None else acc + d
        gu = gu_next
    return acc


def _const_spec(shape, grid_rank):
    zeros = (0,) * len(shape)
    if grid_rank == 1:
        index_map = lambda i: zeros
    elif grid_rank == 2:
        index_map = lambda i, j: zeros
    else:
        index_map = lambda i, j, k: zeros
    return pl.BlockSpec(shape, index_map, pipeline_mode=pl.Buffered(1))


def _row_spec(tm, width):
    return pl.BlockSpec((tm, width), lambda i: (i, 0))


def _rope_kernel(pos_ref, inv_ref, lo_ref, hi_ref, c_ref, slo_ref, shi_ref):
    ang = pos_ref[...].astype(F32) * inv_ref[...]
    sin = jnp.sin(ang)
    c_ref[...] = jnp.cos(ang)
    slo_ref[...] = sin * lo_ref[...]
    shi_ref[...] = sin * hi_ref[...]


def _rope_tables(positions):
    t = positions.size
    tm = min(t, 1024)
    groups = ROPE_TILE // QK_ROPE_DIM
    inv = ROPE_THETA ** (-jnp.arange(0, QK_ROPE_DIM, 2, dtype=F32) / QK_ROPE_DIM)
    ones_h = jnp.ones((ROPE_HALF,), F32)
    zeros_h = jnp.zeros((ROPE_HALF,), F32)
    inv_row = jnp.tile(inv, 2 * groups)[None, :]
    lo = jnp.tile(jnp.concatenate([-ones_h, zeros_h]), groups)[None, :]
    hi = jnp.tile(jnp.concatenate([zeros_h, ones_h]), groups)[None, :]
    row = _const_spec((1, ROPE_TILE), 1)
    return pl.pallas_call(
        _rope_kernel,
        grid=(t // tm,),
        in_specs=[pl.BlockSpec((tm, 1), lambda i: (i, 0)), row, row, row],
        out_specs=[_row_spec(tm, ROPE_TILE)] * 3,
        out_shape=[jax.ShapeDtypeStruct((t, ROPE_TILE), F32)] * 3,
        compiler_params=pltpu.CompilerParams(dimension_semantics=("parallel",)),
        name="rope_tables",
    )(positions.reshape(t, 1), inv_row, lo, hi)


def _rotate(x, c, s_lo, s_hi):
    ahead = pltpu.roll(x, shift=ROPE_TILE - ROPE_HALF, axis=1)
    behind = pltpu.roll(x, shift=ROPE_HALF, axis=1)
    return x * c + ahead * s_lo + behind * s_hi


def _pre_kernel(x_ref, n1_ref, wi_ref, wo_ref, nm_ref, win_ref, qn_ref, kn_ref,
                wqn_ref, wqr_ref, wk_ref, wv_ref, c_ref, slo_ref, shi_ref,
                h_ref, z_ref, xbc_ref, dt_ref, q_ref, k_ref, v_ref):
    x = x_ref[...]
    xn = _rms(x, n1_ref[...]).astype(BF16)
    h1 = x + 0.5 * _ffn(xn, wi_ref, wo_ref)
    h_ref[...] = h1
    u = _rms(h1, nm_ref[...]).astype(BF16)
    z_ref[...] = _dot(u, win_ref[:, _C_Z:_C_XBC])
    xbc_ref[...] = _dot(u, win_ref[:, _C_XBC:_C_QA])
    tail = _dot(u, win_ref[:, _C_QA:_C_END])
    dt_ref[...] = tail[:, _C_DT - _C_QA:_C_KR - _C_QA]
    qn = _rms(tail[:, :_C_KVA - _C_QA], qn_ref[...]).astype(BF16)
    kvn = _rms(tail[:, _C_KVA - _C_QA:_C_DT - _C_QA], kn_ref[...]).astype(BF16)
    c, s_lo, s_hi = c_ref[...], slo_ref[...], shi_ref[...]
    lane = lax.broadcasted_iota(jnp.int32, (1, ROPE_TILE), 1)
    first_group = lane < QK_ROPE_DIM
    kr = _rotate(tail[:, _C_KR - _C_QA:], c, s_lo, s_hi).astype(BF16)
    q_scale = SOFTMAX_SCALE * LOG2_E
    q_nope = _dot(qn, wqn_ref[...]) * q_scale
    k_nope = _dot(kvn, wk_ref[...])
    q_rope = _dot(qn, wqr_ref[...])
    for pair in range(MLA_HEADS // 2):
        tile = slice(pair * ROPE_TILE, (pair + 1) * ROPE_TILE)
        rot = _rotate(q_rope[:, tile], c, s_lo, s_hi) * q_scale
        for sub, part in enumerate((jnp.where(first_group, rot, 0.0),
                                    jnp.where(first_group, pltpu.roll(rot, shift=QK_ROPE_DIM, axis=1), 0.0))):
            base = (2 * pair + sub) * QK_PAD
            q_ref[:, base + QK_NOPE_DIM:base + QK_PAD] = part.astype(BF16)
    for h in range(MLA_HEADS):
        nope = slice(h * QK_NOPE_DIM, (h + 1) * QK_NOPE_DIM)
        q_ref[:, h * QK_PAD:h * QK_PAD + QK_NOPE_DIM] = q_nope[:, nope].astype(BF16)
        k_ref[:, h * QK_PAD:h * QK_PAD + QK_NOPE_DIM] = k_nope[:, nope].astype(BF16)
        k_ref[:, h * QK_PAD + QK_NOPE_DIM:(h + 1) * QK_PAD] = kr
    v_ref[0] = lax.dot_general(wv_ref[...], kvn, (((1,), (1,)), ((), ())),
                               preferred_element_type=F32).astype(BF16)


def _pre_call(h, n1, wi, wo, nm, win, qn, kn, wqn, wqr, wk, wv, rope):
    t = h.shape[0]
    tm = ATTN_TK
    consts = [n1, wi, wo, nm, win, qn, kn, wqn, wqr, wk, wv]
    in_specs = ([_row_spec(tm, D_MODEL)] + [_const_spec(c.shape, 1) for c in consts]
                + [_row_spec(tm, ROPE_TILE)] * len(rope))
    widths = [(D_MODEL, F32), (D_SSM, F32), (CONV_DIM, F32), (DT_PAD, F32),
              (MLA_HEADS * QK_PAD, BF16), (MLA_HEADS * QK_PAD, BF16)]
    return pl.pallas_call(
        _pre_kernel,
        grid=(t // tm,),
        in_specs=in_specs,
        out_specs=[_row_spec(tm, w) for w, _ in widths]
        + [pl.BlockSpec((1, D_MLA, tm), lambda i: (i, 0, 0))],
        out_shape=[jax.ShapeDtypeStruct((t, w), d) for w, d in widths]
        + [jax.ShapeDtypeStruct((t // tm, D_MLA, tm), BF16)],
        compiler_params=pltpu.CompilerParams(
            dimension_semantics=("parallel",), vmem_limit_bytes=V7X_VMEM_LIMIT_BYTES),
        name="pre",
    )(h, *consts, *rope)


def _split_dot(v, m, parts, left=False):
    out = None
    rem = v
    for _ in range(parts):
        piece = rem.astype(BF16)
        rem = rem - piece.astype(F32)
        d = _dot(m, piece) if left else _dot(piece, m)
        out = d if out is None else out + d
    return out


def _softplus(x):
    return jnp.maximum(x, 0.0) + jnp.log1p(jnp.exp(-jnp.abs(x)))


def _ssd_kernel(xbc_ref, z_ref, dt_ref, cw_ref, cb_ref, dtb_ref, alog_ref, dskip_ref, nw_ref,
                tri_ref, exp_ref, y_ref, state_ref, hist_ref):
    q = SSD_BLOCK
    hp = SSD_HEADS_PER_GROUP * SSD_HEAD_DIM

    @pl.when(pl.program_id(1) == 0)
    def _():
        state_ref[...] = jnp.zeros_like(state_ref)
        hist_ref[...] = jnp.zeros_like(hist_ref)

    xr = xbc_ref[...]
    xpad = jnp.concatenate([hist_ref[...], xr], axis=0)
    conv = cb_ref[...] + cw_ref[CONV_WIDTH - 1:CONV_WIDTH, :] * xr
    for back in range(1, CONV_WIDTH):
        shifted = pltpu.roll(xpad, shift=back, axis=0)[CONV_HIST:, :]
        conv = conv + cw_ref[CONV_WIDTH - 1 - back:CONV_WIDTH - back, :] * shifted
    hist_ref[...] = xr[q - CONV_HIST:q, :]
    xbc = conv * jax.nn.sigmoid(conv)
    xs = xbc[:, :D_SSM]

    lane = lax.broadcasted_iota(jnp.int32, (1, DT_PAD), 1)
    head_lane = lane < SSD_HEADS
    dt = jnp.where(head_lane, _softplus(dt_ref[...] + dtb_ref[...]), 0.0)
    a_neg = jnp.where(head_lane, -jnp.exp(alog_ref[...]), 0.0)
    a_cum = _split_dot(dt * (a_neg * LOG2_E), tri_ref[...], 3, left=True)
    a_cum_t = a_cum.T
    a_last = a_cum[q - 1:q, :]
    to_end = jnp.exp2(a_last - a_cum)
    from_start = jnp.exp2(a_cum)

    def expand(v):
        hi = v.astype(BF16)
        lo = (v - hi.astype(F32)).astype(BF16)
        return _dot(jnp.concatenate([hi, lo], axis=1), exp_ref[...])

    dt_e = expand(dt)
    w_e = expand(dt * to_end)
    fs_e = expand(from_start)
    xdt = (xs * dt_e).astype(BF16)
    xw = (xs * w_e).astype(BF16)

    row = lax.broadcasted_iota(jnp.int32, (q, q), 0)
    col = lax.broadcasted_iota(jnp.int32, (q, q), 1)
    causal = col <= row
    lane_pair = lax.broadcasted_iota(jnp.int32, (1, 2 * SSD_HEAD_DIM), 1)
    first_of_pair = lane_pair < SSD_HEAD_DIM

    z = z_ref[...]
    for g in range(SSD_GROUPS):
        b_g = xbc[:, D_SSM + g * SSD_STATE:D_SSM + (g + 1) * SSD_STATE]
        c_g = xbc[:, D_SSM + (SSD_GROUPS + g) * SSD_STATE:D_SSM + (SSD_GROUPS + g + 1) * SSD_STATE]
        b_bf = b_g.astype(BF16)
        c_bf = c_g.astype(BF16)
        cb = lax.dot_general(c_bf, b_bf, (((1,), (1,)), ((), ())), preferred_element_type=F32)
        gsl = slice(g * hp, (g + 1) * hp)
        state = state_ref[g]
        y_g = _dot(c_bf, state.astype(BF16)) * fs_e[:, gsl] + xs[:, gsl] * dskip_ref[:, gsl]
        pieces = []
        for pair in range(SSD_HEADS_PER_GROUP // 2):
            outs = []
            for sub in range(2):
                h = g * SSD_HEADS_PER_GROUP + 2 * pair + sub
                seg = a_cum[:, h:h + 1] - a_cum_t[h:h + 1, :]
                m = (cb * jnp.exp2(jnp.where(causal, seg, NEG_BIG))).astype(BF16)
                psl = slice(g * hp + pair * 2 * SSD_HEAD_DIM, g * hp + (pair + 1) * 2 * SSD_HEAD_DIM)
                outs.append(_dot(m, xdt[:, psl]))
            pieces.append(jnp.where(first_of_pair, outs[0], outs[1]))
        y_g = y_g + jnp.concatenate(pieces, axis=-1)
        y_g = y_g * (z[:, gsl] * jax.nn.sigmoid(z[:, gsl]))
        y_g = y_g * lax.rsqrt(jnp.mean(y_g * y_g, axis=-1, keepdims=True) + EPS) * nw_ref[:, gsl]
        y_ref[:, gsl] = y_g.astype(y_ref.dtype)
        state_ref[g] = state * fs_e[q - 1:q, gsl] + _dot(b_g.T.astype(BF16), xw[:, gsl])


def _ssd_call(xbc, z, dtm, cw, cb, dtb, alog, dskip, nw, tri, expand, batch, seq):
    q = SSD_BLOCK
    nblk = seq // q
    hp = SSD_HEADS_PER_GROUP * SSD_HEAD_DIM
    blk = lambda w: pl.BlockSpec((q, w), lambda b, c: (b * nblk + c, 0))
    consts = [cw, cb, dtb, alog, dskip, nw, tri, expand]
    return pl.pallas_call(
        _ssd_kernel,
        grid=(batch, nblk),
        in_specs=[blk(CONV_DIM), blk(D_SSM), blk(DT_PAD)] + [_const_spec(c.shape, 2) for c in consts],
        out_specs=blk(D_SSM),
        out_shape=jax.ShapeDtypeStruct((batch * seq, D_SSM), BF16),
        scratch_shapes=[pltpu.VMEM((SSD_GROUPS, SSD_STATE, hp), F32),
                        pltpu.VMEM((CONV_HIST, CONV_DIM), F32)],
        compiler_params=pltpu.CompilerParams(
            dimension_semantics=("arbitrary", "arbitrary"), vmem_limit_bytes=V7X_VMEM_LIMIT_BYTES),
        name="ssd",
    )(xbc, z, dtm, *consts)


def _attn_kernel(q_ref, k_ref, vt_ref, o_ref, m_ref, l_ref, acc_ref):
    tq, tk = q_ref.shape[0], ATTN_TK
    qi = pl.program_id(2)
    heads = range(ATTN_HEADS_PER_STEP)
    ones = jnp.ones((ATTN_ONES_ROWS, tk), BF16)

    def qk(g, j, tiles, q_lo):
        start = pl.multiple_of(j * tk, tk)
        k = k_ref[pl.ds(start, tiles * tk), g * QK_PAD:(g + 1) * QK_PAD]
        q = q_ref[q_lo:, g * QK_PAD:(g + 1) * QK_PAD]
        return lax.dot_general(k, q, (((1,), (1,)), ((), ())), preferred_element_type=F32)

    def fold(g, s, j, tiles, q_lo, mask, first):
        if mask is not None:
            s = jnp.where(mask, s, NEG_BIG)
        m_blk = jnp.max(s, axis=0, keepdims=True)
        if first:
            m_new = m_blk
        else:
            m_old = m_ref[g, :, q_lo:]
            m_new = jnp.maximum(m_old, m_blk)
            alpha = jnp.exp2(m_old - m_new)
        p = jnp.exp2(s - m_new).astype(BF16)
        pv = None
        for i in range(tiles):
            vt = jnp.concatenate([vt_ref[j + i, g * V_HEAD_DIM:(g + 1) * V_HEAD_DIM, :], ones], axis=0)
            d = _dot(vt, p[i * tk:(i + 1) * tk, :])
            pv = d if pv is None else pv + d
        p_sum = pv[V_HEAD_DIM:V_HEAD_DIM + 1, :]
        pv = pv[:V_HEAD_DIM, :]
        if first:
            l_ref[g, :, q_lo:] = p_sum
            acc_ref[g, :, q_lo:] = pv
        else:
            l_ref[g, :, q_lo:] = alpha * l_ref[g, :, q_lo:] + p_sum
            acc_ref[g, :, q_lo:] = alpha * acc_ref[g, :, q_lo:] + pv
        m_ref[g, :, q_lo:] = m_new

    def sweep(j, tiles, q_lo, mask, first):
        scores = [qk(g, j, tiles, q_lo) for g in heads]
        for g in heads:
            fold(g, scores[g], j, tiles, q_lo, mask, first)

    for d in range(tq // tk):
        q_lo = d * tk
        key = lax.broadcasted_iota(jnp.int32, (tk, tq - q_lo), 0)
        query = lax.broadcasted_iota(jnp.int32, (tk, tq - q_lo), 1)
        sweep(qi * (tq // tk) + d, 1, q_lo, key // CHUNK <= query // CHUNK, first=(d == 0))

    @pl.loop(0, qi * (tq // tk) // ATTN_LOOP_TILES)
    def _(trip):
        sweep(trip * ATTN_LOOP_TILES, ATTN_LOOP_TILES, 0, None, first=False)

    for g in heads:
        o_ref[:, g * V_HEAD_DIM:(g + 1) * V_HEAD_DIM] = (acc_ref[g] / l_ref[g]).T.astype(o_ref.dtype)


def _attn_call(q, k, vt, batch, seq):
    tq = min(seq, ATTN_TQ)
    nq = seq // tq
    g = ATTN_HEADS_PER_STEP
    return pl.pallas_call(
        _attn_kernel,
        grid=(batch, MLA_HEADS // g, nq),
        in_specs=[pl.BlockSpec((tq, g * QK_PAD), lambda b, h, i: (b * nq + i, h)),
                  pl.BlockSpec((seq, g * QK_PAD), lambda b, h, i: (b, h)),
                  pl.BlockSpec((seq // ATTN_TK, g * V_HEAD_DIM, ATTN_TK), lambda b, h, i: (b, h, 0))],
        out_specs=pl.BlockSpec((tq, g * V_HEAD_DIM), lambda b, h, i: (b * nq + i, h)),
        out_shape=jax.ShapeDtypeStruct((batch * seq, D_MLA), BF16),
        scratch_shapes=[pltpu.VMEM((g, 1, tq), F32), pltpu.VMEM((g, 1, tq), F32),
                        pltpu.VMEM((g, V_HEAD_DIM, tq), F32)],
        compiler_params=pltpu.CompilerParams(
            dimension_semantics=("parallel", "parallel", "arbitrary"),
            vmem_limit_bytes=V7X_VMEM_LIMIT_BYTES),
        name="attn",
    )(q, k, vt)


def _post_kernel(h_ref, ys_ref, ym_ref, p_ref, wmix_ref, n2_ref, wi_ref, wo_ref,
                 np_ref, wgate_ref, wproj_ref, nf_ref, o_ref, *, final):
    h2 = (h_ref[...] + _dot(ys_ref[...], wmix_ref[:D_SSM, :]) + _dot(ym_ref[...], wmix_ref[D_SSM:, :]))
    xn = _rms(h2, n2_ref[...]).astype(BF16)
    h3 = h2 + 0.5 * _ffn(xn, wi_ref, wo_ref)
    gate = jax.nn.sigmoid(_dot(_rms(h3, np_ref[...]).astype(BF16), wgate_ref[...]))
    h4 = h3 + gate * _dot(p_ref[...].astype(BF16), wproj_ref[...])
    if final:
        h4 = _rms(h4, nf_ref[...])
    o_ref[...] = h4


def _post_call(h, ys, ym, p, wmix, n2, wi, wo, npl, wgate, wproj, nf, final):
    t = h.shape[0]
    tm = min(t, TOKEN_TILE)
    consts = [wmix, n2, wi, wo, npl, wgate, wproj, nf]
    in_specs = ([_row_spec(tm, D_MODEL), _row_spec(tm, D_SSM), _row_spec(tm, D_MLA),
                 _row_spec(tm, PLE_DIM)] + [_const_spec(c.shape, 1) for c in consts])
    return pl.pallas_call(
        functools.partial(_post_kernel, final=final),
        grid=(t // tm,),
        in_specs=in_specs,
        out_specs=_row_spec(tm, D_MODEL),
        out_shape=jax.ShapeDtypeStruct((t, D_MODEL), F32),
        compiler_params=pltpu.CompilerParams(
            dimension_semantics=("parallel",), vmem_limit_bytes=V7X_VMEM_LIMIT_BYTES),
        name="post",
    )(h, ys, ym, p, *consts)


def _pad_cols(w, lo, width):
    return jnp.pad(w, ((0, 0), (lo, width - lo - w.shape[1])))


def _layout_in_proj(w):
    o = 0
    w_z = w[:, o:o + D_SSM]; o += D_SSM
    w_xbc = w[:, o:o + CONV_DIM]; o += CONV_DIM
    w_dt = w[:, o:o + SSD_HEADS]; o += SSD_HEADS
    w_qa = w[:, o:o + Q_LORA_RANK]; o += Q_LORA_RANK
    w_kva = w[:, o:o + KV_LORA_RANK]; o += KV_LORA_RANK
    w_kr = w[:, o:o + QK_ROPE_DIM]
    return jnp.concatenate([
        w_z, w_xbc, w_qa, w_kva, _pad_cols(w_dt, 0, DT_PAD), _pad_cols(w_kr, 0, KR_PAD)],
        axis=1).astype(BF16)


def _layout_q_b(w):
    w = w.reshape(Q_LORA_RANK, MLA_HEADS, QK_NOPE_DIM + QK_ROPE_DIM)
    return (w[:, :, :QK_NOPE_DIM].reshape(Q_LORA_RANK, MLA_HEADS * QK_NOPE_DIM).astype(BF16),
            w[:, :, QK_NOPE_DIM:].reshape(Q_LORA_RANK, MLA_HEADS * QK_ROPE_DIM).astype(BF16))


def _layout_kv_b(w):
    w = w.reshape(KV_LORA_RANK, MLA_HEADS, QK_NOPE_DIM + V_HEAD_DIM)
    wk = w[:, :, :QK_NOPE_DIM].reshape(KV_LORA_RANK, MLA_HEADS * QK_NOPE_DIM)
    wv_t = w[:, :, QK_NOPE_DIM:].reshape(KV_LORA_RANK, D_MLA).T
    return wk.astype(BF16), wv_t.astype(BF16)


def _row(v):
    return v.reshape(1, -1).astype(F32)


def kernel(x, p, positions, ffn1_norm, ffn1_w_in, ffn1_w_out, mix_norm, w_in_mix, conv_w, conv_b,
           dt_bias, a_log, d_skip, ssd_norm, q_a_norm, w_q_b, kv_a_norm, w_kv_b, w_out_mix,
           ffn2_norm, ffn2_w_in, ffn2_w_out, ple_norm, w_ple_gate, w_ple_proj, final_norm):
    batch, seq, _ = x.shape
    depth = p.shape[0]
    t = batch * seq
    assert seq % SSD_BLOCK == 0 and seq % min(seq, ATTN_TQ) == 0 and t % TOKEN_TILE == 0

    rope = _rope_tables(positions)
    q_idx = jnp.arange(SSD_BLOCK)
    tri = (q_idx[None, :] <= q_idx[:, None]).astype(BF16)
    expand = (jnp.arange(D_SSM)[None, :] // SSD_HEAD_DIM == jnp.arange(DT_PAD)[:, None]).astype(BF16)
    expand = jnp.concatenate([expand, expand], axis=0)

    h = x.reshape(t, D_MODEL)
    for i in range(depth):
        wqn, wqr = _layout_q_b(w_q_b[i])
        wk, wv = _layout_kv_b(w_kv_b[i])
        h, z, xbc, dtm, q, k, v = _pre_call(
            h, _row(ffn1_norm[i]), ffn1_w_in[i].astype(BF16),
            ffn1_w_out[i].astype(BF16), _row(mix_norm[i]), _layout_in_proj(w_in_mix[i]),
            _row(q_a_norm[i]), _row(kv_a_norm[i]), wqn, wqr, wk, wv, rope)
        y_ssd = _ssd_call(
            xbc, z, dtm, conv_w[i].astype(F32), _row(conv_b[i]),
            _pad_cols(_row(dt_bias[i]), 0, DT_PAD), _pad_cols(_row(a_log[i]), 0, DT_PAD),
            _row(jnp.repeat(d_skip[i], SSD_HEAD_DIM)), _row(ssd_norm[i]), tri, expand, batch, seq)
        y_mla = _attn_call(q, k, v, batch, seq)
        h = _post_call(
            h, y_ssd, y_mla, p[i].reshape(t, PLE_DIM),
            w_out_mix[i].astype(BF16), _row(ffn2_norm[i]), ffn2_w_in[i].astype(BF16),
            ffn2_w_out[i].astype(BF16), _row(ple_norm[i]), w_ple_gate[i].astype(BF16),
            w_ple_proj[i].astype(BF16), _row(final_norm), final=(i == depth - 1))
    return h.reshape(batch, seq, D_MODEL)
```

```python
import functools

import jax
import jax.numpy as jnp
from jax import lax
from jax.experimental import pallas as pl
from jax.experimental.pallas import tpu as pltpu

F32 = jnp.float32
BF16 = jnp.bfloat16

D_MODEL = 1024
CHUNK = 64
PLE_DIM = 256
D_FF = 2816
EPS = 1e-6
SSD_HEADS = 16
SSD_HEAD_DIM = 64
D_SSM = SSD_HEADS * SSD_HEAD_DIM
SSD_GROUPS = 2
SSD_HEADS_PER_GROUP = SSD_HEADS // SSD_GROUPS
SSD_STATE = 128
CONV_WIDTH = 4
CONV_DIM = D_SSM + 2 * SSD_GROUPS * SSD_STATE
MLA_HEADS = 8
QK_NOPE_DIM = 128
QK_ROPE_DIM = 64
V_HEAD_DIM = 128
Q_LORA_RANK = 384
KV_LORA_RANK = 256
D_MLA = MLA_HEADS * V_HEAD_DIM
ROPE_THETA = 10000.0
SOFTMAX_SCALE = (QK_NOPE_DIM + QK_ROPE_DIM) ** -0.5
LOG2_E = 1.4426950408889634

V7X_LANES = 128
V7X_SUBLANES = 8
V7X_MXU_DIM = 256
V7X_VMEM_LIMIT_BYTES = 56 * 1024 * 1024

QK_PAD = V7X_MXU_DIM
ROPE_HALF = QK_ROPE_DIM // 2
ROPE_TILE = V7X_LANES
DT_PAD = V7X_LANES
KR_PAD = V7X_LANES
FF_CHUNK = V7X_MXU_DIM
SSD_BLOCK = 256
CONV_HIST = V7X_SUBLANES
TOKEN_TILE = 512
ATTN_TQ = 512
ATTN_TK = 256
ATTN_HEADS_PER_STEP = 4
ATTN_LOOP_TILES = 2
ATTN_ONES_ROWS = 2 * V7X_SUBLANES
NEG_BIG = -1e30

_C_Z = 0
_C_XBC = _C_Z + D_SSM
_C_QA = _C_XBC + CONV_DIM
_C_KVA = _C_QA + Q_LORA_RANK
_C_DT = _C_KVA + KV_LORA_RANK
_C_KR = _C_DT + DT_PAD
_C_END = _C_KR + KR_PAD


def _dot(a, b):
    return jnp.dot(a, b, preferred_element_type=F32)


def _rms(x, w):
    return x * lax.rsqrt(jnp.mean(x * x, axis=-1, keepdims=True) + EPS) * w


def _ffn(xn, wi_ref, wo_ref):
    n = D_FF // FF_CHUNK
    cols = lambda c: slice(c * FF_CHUNK, (c + 1) * FF_CHUNK)
    up_cols = lambda c: slice(D_FF + c * FF_CHUNK, D_FF + (c + 1) * FF_CHUNK)
    up = lambda c: (_dot(xn, wi_ref[:, cols(c)]), _dot(xn, wi_ref[:, up_cols(c)]))
    acc = None
    gu = up(0)
    for c in range(n):
        gu_next = up(c + 1) if c + 1 < n else None
        g, u = gu
        a = (g * jax.nn.sigmoid(g) * u).astype(BF16)
        d = _dot(a, wo_ref[cols(c), :])
        acc = d if acc is None else acc + d
        gu = gu_next
    return acc


def _const_spec(shape, grid_rank):
    zeros = (0,) * len(shape)
    if grid_rank == 1:
        index_map = lambda i: zeros
    elif grid_rank == 2:
        index_map = lambda i, j: zeros
    else:
        index_map = lambda i, j, k: zeros
    return pl.BlockSpec(shape, index_map, pipeline_mode=pl.Buffered(1))


def _row_spec(tm, width):
    return pl.BlockSpec((tm, width), lambda i: (i, 0))


def _rope_kernel(pos_ref, inv_ref, lo_ref, hi_ref, c_ref, slo_ref, shi_ref):
    ang = pos_ref[...].astype(F32) * inv_ref[...]
    sin = jnp.sin(ang)
    c_ref[...] = jnp.cos(ang)
    slo_ref[...] = sin * lo_ref[...]
    shi_ref[...] = sin * hi_ref[...]


def _rope_tables(positions):
    t = positions.size
    tm = min(t, 1024)
    groups = ROPE_TILE // QK_ROPE_DIM
    inv = ROPE_THETA ** (-jnp.arange(0, QK_ROPE_DIM, 2, dtype=F32) / QK_ROPE_DIM)
    ones_h = jnp.ones((ROPE_HALF,), F32)
    zeros_h = jnp.zeros((ROPE_HALF,), F32)
    inv_row = jnp.tile(inv, 2 * groups)[None, :]
    lo = jnp.tile(jnp.concatenate([-ones_h, zeros_h]), groups)[None, :]
    hi = jnp.tile(jnp.concatenate([zeros_h, ones_h]), groups)[None, :]
    row = _const_spec((1, ROPE_TILE), 1)
    return pl.pallas_call(
        _rope_kernel,
        grid=(t // tm,),
        in_specs=[pl.BlockSpec((tm, 1), lambda i: (i, 0)), row, row, row],
        out_specs=[_row_spec(tm, ROPE_TILE)] * 3,
        out_shape=[jax.ShapeDtypeStruct((t, ROPE_TILE), F32)] * 3,
        compiler_params=pltpu.CompilerParams(dimension_semantics=("parallel",)),
        name="rope_tables",
    )(positions.reshape(t, 1), inv_row, lo, hi)


def _rotate(x, c, s_lo, s_hi):
    ahead = pltpu.roll(x, shift=ROPE_TILE - ROPE_HALF, axis=1)
    behind = pltpu.roll(x, shift=ROPE_HALF, axis=1)
    return x * c + ahead * s_lo + behind * s_hi


def _silu(x):
    return x * jax.nn.sigmoid(x)


def _pre_kernel(x_ref, n1_ref, wi_ref, wo_ref, nm_ref, win_ref, qn_ref, kn_ref,
                wqn_ref, wqr_ref, wk_ref, wv_ref, cw_ref, cb_ref, c_ref, slo_ref, shi_ref,
                h_ref, zg_ref, xbc_ref, dt_ref, q_ref, k_ref, v_ref, raw_ref, hist_ref,
                *, tiles_per_seq):
    i = pl.program_id(0)

    @pl.when(i == 0)
    def _():
        raw_ref[...] = jnp.zeros_like(raw_ref)
        hist_ref[...] = jnp.zeros_like(hist_ref)

    prev = raw_ref[...]
    seq_start = lax.rem(i - 1, tiles_per_seq) == 0
    hist = jnp.where(seq_start, 0.0, hist_ref[...])
    xpad = jnp.concatenate([hist, prev], axis=0)
    conv = cb_ref[...] + cw_ref[CONV_WIDTH - 1:CONV_WIDTH, :] * prev
    for back in range(1, CONV_WIDTH):
        shifted = pltpu.roll(xpad, shift=back, axis=0)[CONV_HIST:, :]
        conv = conv + cw_ref[CONV_WIDTH - 1 - back:CONV_WIDTH - back, :] * shifted
    xbc_ref[...] = _silu(conv)
    hist_ref[...] = prev[prev.shape[0] - CONV_HIST:, :]

    x = x_ref[...]
    xn = _rms(x, n1_ref[...]).astype(BF16)
    h1 = x + 0.5 * _ffn(xn, wi_ref, wo_ref)
    h_ref[...] = h1
    u = _rms(h1, nm_ref[...]).astype(BF16)
    tail = _dot(u, win_ref[:, _C_QA:_C_END])
    dt_ref[...] = tail[:, _C_DT - _C_QA:_C_KR - _C_QA]
    qn = _rms(tail[:, :_C_KVA - _C_QA], qn_ref[...]).astype(BF16)
    kvn = _rms(tail[:, _C_KVA - _C_QA:_C_DT - _C_QA], kn_ref[...]).astype(BF16)
    c, s_lo, s_hi = c_ref[...], slo_ref[...], shi_ref[...]
    lane = lax.broadcasted_iota(jnp.int32, (1, ROPE_TILE), 1)
    first_group = lane < QK_ROPE_DIM
    kr = _rotate(tail[:, _C_KR - _C_QA:], c, s_lo, s_hi).astype(BF16)
    q_scale = SOFTMAX_SCALE * LOG2_E
    q_nope = _dot(qn, wqn_ref[...]) * q_scale
    k_nope = _dot(kvn, wk_ref[...])
    q_rope = _dot(qn, wqr_ref[...])
    for pair in range(MLA_HEADS // 2):
        tile = slice(pair * ROPE_TILE, (pair + 1) * ROPE_TILE)
        rot = _rotate(q_rope[:, tile], c, s_lo, s_hi) * q_scale
        for sub, part in enumerate((jnp.where(first_group, rot, 0.0),
                                    jnp.where(first_group, pltpu.roll(rot, shift=QK_ROPE_DIM, axis=1), 0.0))):
            base = (2 * pair + sub) * QK_PAD
            q_ref[:, base + QK_NOPE_DIM:base + QK_PAD] = part.astype(BF16)
    for h in range(MLA_HEADS):
        nope = slice(h * QK_NOPE_DIM, (h + 1) * QK_NOPE_DIM)
        q_ref[:, h * QK_PAD:h * QK_PAD + QK_NOPE_DIM] = q_nope[:, nope].astype(BF16)
        k_ref[:, h * QK_PAD:h * QK_PAD + QK_NOPE_DIM] = k_nope[:, nope].astype(BF16)
        k_ref[:, h * QK_PAD + QK_NOPE_DIM:(h + 1) * QK_PAD] = kr
    v_ref[0] = lax.dot_general(wv_ref[...], kvn, (((1,), (1,)), ((), ())),
                               preferred_element_type=F32).astype(BF16)
    zg_ref[...] = _silu(_dot(u, win_ref[:, _C_Z:_C_XBC]))
    raw_ref[...] = _dot(u, win_ref[:, _C_XBC:_C_QA])


def _pre_call(h, n1, wi, wo, nm, win, qn, kn, wqn, wqr, wk, wv, cw, cb, rope, seq):
    t = h.shape[0]
    tm = ATTN_TK
    n = t // tm
    consts = [n1, wi, wo, nm, win, qn, kn, wqn, wqr, wk, wv, cw, cb]
    cur = lambda i: (jnp.minimum(i, n - 1), 0)
    prev = lambda i: (jnp.maximum(i - 1, 0), 0)
    in_specs = ([pl.BlockSpec((tm, D_MODEL), cur)] + [_const_spec(c.shape, 1) for c in consts]
                + [pl.BlockSpec((tm, ROPE_TILE), cur)] * len(rope))
    outs = [(D_MODEL, F32, cur), (D_SSM, F32, cur), (CONV_DIM, F32, prev), (DT_PAD, F32, cur),
            (MLA_HEADS * QK_PAD, BF16, cur), (MLA_HEADS * QK_PAD, BF16, cur)]
    return pl.pallas_call(
        functools.partial(_pre_kernel, tiles_per_seq=seq // tm),
        grid=(n + 1,),
        in_specs=in_specs,
        out_specs=[pl.BlockSpec((tm, w), im) for w, _, im in outs]
        + [pl.BlockSpec((1, D_MLA, tm), lambda i: (jnp.minimum(i, n - 1), 0, 0))],
        out_shape=[jax.ShapeDtypeStruct((t, w), d) for w, d, _ in outs]
        + [jax.ShapeDtypeStruct((n, D_MLA, tm), BF16)],
        scratch_shapes=[pltpu.VMEM((tm, CONV_DIM), F32), pltpu.VMEM((CONV_HIST, CONV_DIM), F32)],
        compiler_params=pltpu.CompilerParams(
            dimension_semantics=("arbitrary",), vmem_limit_bytes=V7X_VMEM_LIMIT_BYTES),
        name="pre",
    )(h, *consts, *rope)


def _split_dot(v, m, parts, left=False):
    out = None
    rem = v
    for _ in range(parts):
        piece = rem.astype(BF16)
        rem = rem - piece.astype(F32)
        d = _dot(m, piece) if left else _dot(piece, m)
        out = d if out is None else out + d
    return out


def _softplus(x):
    return jnp.maximum(x, 0.0) + jnp.log1p(jnp.exp(-jnp.abs(x)))


def _ssd_kernel(xbc_ref, zg_ref, dt_ref, dtb_ref, alog_ref, dskip_ref, nw_ref,
                tri_ref, exp_ref, y_ref, state_ref):
    q = SSD_BLOCK
    hp = SSD_HEADS_PER_GROUP * SSD_HEAD_DIM

    @pl.when(pl.program_id(1) == 0)
    def _():
        state_ref[...] = jnp.zeros_like(state_ref)

    xbc = xbc_ref[...]
    xs = xbc[:, :D_SSM]

    lane = lax.broadcasted_iota(jnp.int32, (1, DT_PAD), 1)
    head_lane = lane < SSD_HEADS
    dt = jnp.where(head_lane, _softplus(dt_ref[...] + dtb_ref[...]), 0.0)
    a_neg = jnp.where(head_lane, -jnp.exp(alog_ref[...]), 0.0)
    a_cum = _split_dot(dt * (a_neg * LOG2_E), tri_ref[...], 3, left=True)
    a_cum_t = a_cum.T
    a_last = a_cum[q - 1:q, :]
    to_end = jnp.exp2(a_last - a_cum)
    from_start = jnp.exp2(a_cum)

    def expand(v):
        hi = v.astype(BF16)
        lo = (v - hi.astype(F32)).astype(BF16)
        return _dot(jnp.concatenate([hi, lo], axis=1), exp_ref[...])

    dt_e = expand(dt)
    w_e = expand(dt * to_end)
    fs_e = expand(from_start)
    xdt = (xs * dt_e).astype(BF16)
    xw = (xs * w_e).astype(BF16)

    row = lax.broadcasted_iota(jnp.int32, (q, q), 0)
    col = lax.broadcasted_iota(jnp.int32, (q, q), 1)
    causal = col <= row
    lane_pair = lax.broadcasted_iota(jnp.int32, (1, 2 * SSD_HEAD_DIM), 1)
    first_of_pair = lane_pair < SSD_HEAD_DIM

    zg = zg_ref[...]
    for g in range(SSD_GROUPS):
        b_g = xbc[:, D_SSM + g * SSD_STATE:D_SSM + (g + 1) * SSD_STATE]
        c_g = xbc[:, D_SSM + (SSD_GROUPS + g) * SSD_STATE:D_SSM + (SSD_GROUPS + g + 1) * SSD_STATE]
        b_bf = b_g.astype(BF16)
        c_bf = c_g.astype(BF16)
        cb = lax.dot_general(c_bf, b_bf, (((1,), (1,)), ((), ())), preferred_element_type=F32)
        gsl = slice(g * hp, (g + 1) * hp)
        state = state_ref[g]
        y_g = _dot(c_bf, state.astype(BF16)) * fs_e[:, gsl] + xs[:, gsl] * dskip_ref[:, gsl]
        pieces = []
        for pair in range(SSD_HEADS_PER_GROUP // 2):
            outs = []
            for sub in range(2):
                h = g * SSD_HEADS_PER_GROUP + 2 * pair + sub
                seg = a_cum[:, h:h + 1] - a_cum_t[h:h + 1, :]
                m = (cb * jnp.exp2(jnp.where(causal, seg, NEG_BIG))).astype(BF16)
                psl = slice(g * hp + pair * 2 * SSD_HEAD_DIM, g * hp + (pair + 1) * 2 * SSD_HEAD_DIM)
                outs.append(_dot(m, xdt[:, psl]))
            pieces.append(jnp.where(first_of_pair, outs[0], outs[1]))
        y_g = y_g + jnp.concatenate(pieces, axis=-1)
        y_g = y_g * zg[:, gsl]
        y_g = y_g * lax.rsqrt(jnp.mean(y_g * y_g, axis=-1, keepdims=True) + EPS) * nw_ref[:, gsl]
        y_ref[:, gsl] = y_g.astype(y_ref.dtype)
        state_ref[g] = state * fs_e[q - 1:q, gsl] + _dot(b_g.T.astype(BF16), xw[:, gsl])


def _ssd_call(xbc, zg, dtm, dtb, alog, dskip, nw, tri, expand, batch, seq):
    q = SSD_BLOCK
    nblk = seq // q
    hp = SSD_HEADS_PER_GROUP * SSD_HEAD_DIM
    blk = lambda w: pl.BlockSpec((q, w), lambda b, c: (b * nblk + c, 0))
    consts = [dtb, alog, dskip, nw, tri, expand]
    return pl.pallas_call(
        _ssd_kernel,
        grid=(batch, nblk),
        in_specs=[blk(CONV_DIM), blk(D_SSM), blk(DT_PAD)] + [_const_spec(c.shape, 2) for c in consts],
        out_specs=blk(D_SSM),
        out_shape=jax.ShapeDtypeStruct((batch * seq, D_SSM), BF16),
        scratch_shapes=[pltpu.VMEM((SSD_GROUPS, SSD_STATE, hp), F32)],
        compiler_params=pltpu.CompilerParams(
            dimension_semantics=("arbitrary", "arbitrary"), vmem_limit_bytes=V7X_VMEM_LIMIT_BYTES),
        name="ssd",
    )(xbc, zg, dtm, *consts)


def _attn_kernel(q_ref, k_ref, vt_ref, o_ref, m_ref, l_ref, acc_ref):
    tq, tk = q_ref.shape[0], ATTN_TK
    qi = pl.program_id(2)
    heads = range(ATTN_HEADS_PER_STEP)
    ones = jnp.ones((ATTN_ONES_ROWS, tk), BF16)

    def qk(g, j, tiles, q_lo):
        start = pl.multiple_of(j * tk, tk)
        k = k_ref[pl.ds(start, tiles * tk), g * QK_PAD:(g + 1) * QK_PAD]
        q = q_ref[q_lo:, g * QK_PAD:(g + 1) * QK_PAD]
        return lax.dot_general(k, q, (((1,), (1,)), ((), ())), preferred_element_type=F32)

    def fold(g, s, j, tiles, q_lo, mask, first):
        if mask is not None:
            s = jnp.where(mask, s, NEG_BIG)
        m_blk = jnp.max(s, axis=0, keepdims=True)
        if first:
            m_new = m_blk
        else:
            m_old = m_ref[g, :, q_lo:]
            m_new = jnp.maximum(m_old, m_blk)
            alpha = jnp.exp2(m_old - m_new)
        p = jnp.exp2(s - m_new).astype(BF16)
        pv = None
        for i in range(tiles):
            vt = jnp.concatenate([vt_ref[j + i, g * V_HEAD_DIM:(g + 1) * V_HEAD_DIM, :], ones], axis=0)
            d = _dot(vt, p[i * tk:(i + 1) * tk, :])
            pv = d if pv is None else pv + d
        p_sum = pv[V_HEAD_DIM:V_HEAD_DIM + 1, :]
        pv = pv[:V_HEAD_DIM, :]
        if first:
            l_ref[g, :, q_lo:] = p_sum
            acc_ref[g, :, q_lo:] = pv
        else:
            l_ref[g, :, q_lo:] = alpha * l_ref[g, :, q_lo:] + p_sum
            acc_ref[g, :, q_lo:] = alpha * acc_ref[g, :, q_lo:] + pv
        m_ref[g, :, q_lo:] = m_new

    def sweep(j, tiles, q_lo, mask, first):
        scores = [qk(g, j, tiles, q_lo) for g in heads]
        for g in heads:
            fold(g, scores[g], j, tiles, q_lo, mask, first)

    for d in range(tq // tk):
        q_lo = d * tk
        key = lax.broadcasted_iota(jnp.int32, (tk, tq - q_lo), 0)
        query = lax.broadcasted_iota(jnp.int32, (tk, tq - q_lo), 1)
        sweep(qi * (tq // tk) + d, 1, q_lo, key // CHUNK <= query // CHUNK, first=(d == 0))

    @pl.loop(0, qi * (tq // tk) // ATTN_LOOP_TILES)
    def _(trip):
        sweep(trip * ATTN_LOOP_TILES, ATTN_LOOP_TILES, 0, None, first=False)

    for g in heads:
        o_ref[:, g * V_HEAD_DIM:(g + 1) * V_HEAD_DIM] = (acc_ref[g] / l_ref[g]).T.astype(o_ref.dtype)


def _attn_call(q, k, vt, batch, seq):
    tq = min(seq, ATTN_TQ)
    nq = seq // tq
    g = ATTN_HEADS_PER_STEP
    return pl.pallas_call(
        _attn_kernel,
        grid=(batch, MLA_HEADS // g, nq),
        in_specs=[pl.BlockSpec((tq, g * QK_PAD), lambda b, h, i: (b * nq + i, h)),
                  pl.BlockSpec((seq, g * QK_PAD), lambda b, h, i: (b, h)),
                  pl.BlockSpec((seq // ATTN_TK, g * V_HEAD_DIM, ATTN_TK), lambda b, h, i: (b, h, 0))],
        out_specs=pl.BlockSpec((tq, g * V_HEAD_DIM), lambda b, h, i: (b * nq + i, h)),
        out_shape=jax.ShapeDtypeStruct((batch * seq, D_MLA), BF16),
        scratch_shapes=[pltpu.VMEM((g, 1, tq), F32), pltpu.VMEM((g, 1, tq), F32),
                        pltpu.VMEM((g, V_HEAD_DIM, tq), F32)],
        compiler_params=pltpu.CompilerParams(
            dimension_semantics=("parallel", "parallel", "arbitrary"),
            vmem_limit_bytes=V7X_VMEM_LIMIT_BYTES),
        name="attn",
    )(q, k, vt)


def _post_kernel(h_ref, ys_ref, ym_ref, p_ref, wmix_ref, n2_ref, wi_ref, wo_ref,
                 np_ref, wgate_ref, wproj_ref, nf_ref, o_ref, *, final):
    h2 = (h_ref[...] + _dot(ys_ref[...], wmix_ref[:D_SSM, :]) + _dot(ym_ref[...], wmix_ref[D_SSM:, :]))
    xn = _rms(h2, n2_ref[...]).astype(BF16)
    h3 = h2 + 0.5 * _ffn(xn, wi_ref, wo_ref)
    gate = jax.nn.sigmoid(_dot(_rms(h3, np_ref[...]).astype(BF16), wgate_ref[...]))
    h4 = h3 + gate * _dot(p_ref[...].astype(BF16), wproj_ref[...])
    if final:
        h4 = _rms(h4, nf_ref[...])
    o_ref[...] = h4


def _post_call(h, ys, ym, p, wmix, n2, wi, wo, npl, wgate, wproj, nf, final):
    t = h.shape[0]
    tm = min(t, TOKEN_TILE)
    consts = [wmix, n2, wi, wo, npl, wgate, wproj, nf]
    in_specs = ([_row_spec(tm, D_MODEL), _row_spec(tm, D_SSM), _row_spec(tm, D_MLA),
                 _row_spec(tm, PLE_DIM)] + [_const_spec(c.shape, 1) for c in consts])
    return pl.pallas_call(
        functools.partial(_post_kernel, final=final),
        grid=(t // tm,),
        in_specs=in_specs,
        out_specs=_row_spec(tm, D_MODEL),
        out_shape=jax.ShapeDtypeStruct((t, D_MODEL), F32),
        compiler_params=pltpu.CompilerParams(
            dimension_semantics=("parallel",), vmem_limit_bytes=V7X_VMEM_LIMIT_BYTES),
        name="post",
    )(h, ys, ym, p, *consts)


def _pad_cols(w, lo, width):
    return jnp.pad(w, ((0, 0), (lo, width - lo - w.shape[1])))


def _layout_in_proj(w):
    o = 0
    w_z = w[:, o:o + D_SSM]; o += D_SSM
    w_xbc = w[:, o:o + CONV_DIM]; o += CONV_DIM
    w_dt = w[:, o:o + SSD_HEADS]; o += SSD_HEADS
    w_qa = w[:, o:o + Q_LORA_RANK]; o += Q_LORA_RANK
    w_kva = w[:, o:o + KV_LORA_RANK]; o += KV_LORA_RANK
    w_kr = w[:, o:o + QK_ROPE_DIM]
    return jnp.concatenate([
        w_z, w_xbc, w_qa, w_kva, _pad_cols(w_dt, 0, DT_PAD), _pad_cols(w_kr, 0, KR_PAD)],
        axis=1).astype(BF16)


def _layout_q_b(w):
    w = w.reshape(Q_LORA_RANK, MLA_HEADS, QK_NOPE_DIM + QK_ROPE_DIM)
    return (w[:, :, :QK_NOPE_DIM].reshape(Q_LORA_RANK, MLA_HEADS * QK_NOPE_DIM).astype(BF16),
            w[:, :, QK_NOPE_DIM:].reshape(Q_LORA_RANK, MLA_HEADS * QK_ROPE_DIM).astype(BF16))


def _layout_kv_b(w):
    w = w.reshape(KV_LORA_RANK, MLA_HEADS, QK_NOPE_DIM + V_HEAD_DIM)
    wk = w[:, :, :QK_NOPE_DIM].reshape(KV_LORA_RANK, MLA_HEADS * QK_NOPE_DIM)
    wv_t = w[:, :, QK_NOPE_DIM:].reshape(KV_LORA_RANK, D_MLA).T
    return wk.astype(BF16), wv_t.astype(BF16)


def _row(v):
    return v.reshape(1, -1).astype(F32)


def kernel(x, p, positions, ffn1_norm, ffn1_w_in, ffn1_w_out, mix_norm, w_in_mix, conv_w, conv_b,
           dt_bias, a_log, d_skip, ssd_norm, q_a_norm, w_q_b, kv_a_norm, w_kv_b, w_out_mix,
           ffn2_norm, ffn2_w_in, ffn2_w_out, ple_norm, w_ple_gate, w_ple_proj, final_norm):
    batch, seq, _ = x.shape
    depth = p.shape[0]
    t = batch * seq
    assert seq % SSD_BLOCK == 0 and seq % min(seq, ATTN_TQ) == 0 and t % TOKEN_TILE == 0

    rope = _rope_tables(positions)
    q_idx = jnp.arange(SSD_BLOCK)
    tri = (q_idx[None, :] <= q_idx[:, None]).astype(BF16)
    expand = (jnp.arange(D_SSM)[None, :] // SSD_HEAD_DIM == jnp.arange(DT_PAD)[:, None]).astype(BF16)
    expand = jnp.concatenate([expand, expand], axis=0)

    h = x.reshape(t, D_MODEL)
    for i in range(depth):
        wqn, wqr = _layout_q_b(w_q_b[i])
        wk, wv = _layout_kv_b(w_kv_b[i])
        h, zg, xbc, dtm, q, k, v = _pre_call(
            h, _row(ffn1_norm[i]), ffn1_w_in[i].astype(BF16),
            ffn1_w_out[i].astype(BF16), _row(mix_norm[i]), _layout_in_proj(w_in_mix[i]),
            _row(q_a_norm[i]), _row(kv_a_norm[i]), wqn, wqr, wk, wv,
            conv_w[i].astype(F32), _row(conv_b[i]), rope, seq)
        y_ssd = _ssd_call(
            xbc, zg, dtm,
            _pad_cols(_row(dt_bias[i]), 0, DT_PAD), _pad_cols(_row(a_log[i]), 0, DT_PAD),
            _row(jnp.repeat(d_skip[i], SSD_HEAD_DIM)), _row(ssd_norm[i]), tri, expand, batch, seq)
        y_mla = _attn_call(q, k, v, batch, seq)
        h = _post_call(
            h, y_ssd, y_mla, p[i].reshape(t, PLE_DIM),
            w_out_mix[i].astype(BF16), _row(ffn2_norm[i]), ffn2_w_in[i].astype(BF16),
            ffn2_w_out[i].astype(BF16), _row(ple_norm[i]), w_ple_gate[i].astype(BF16),
            w_ple_proj[i].astype(BF16), _row(final_norm), final=(i == depth - 1))
    return h.reshape(batch, seq, D_MODEL)
```

```python
import functools

import jax
import jax.numpy as jnp
from jax import lax
from jax.experimental import pallas as pl
from jax.experimental.pallas import tpu as pltpu

F32 = jnp.float32
BF16 = jnp.bfloat16

D_MODEL = 1024
CHUNK = 64
PLE_DIM = 256
D_FF = 2816
EPS = 1e-6
SSD_HEADS = 16
SSD_HEAD_DIM = 64
D_SSM = SSD_HEADS * SSD_HEAD_DIM
SSD_GROUPS = 2
SSD_HEADS_PER_GROUP = SSD_HEADS // SSD_GROUPS
SSD_STATE = 128
CONV_WIDTH = 4
CONV_DIM = D_SSM + 2 * SSD_GROUPS * SSD_STATE
MLA_HEADS = 8
QK_NOPE_DIM = 128
QK_ROPE_DIM = 64
V_HEAD_DIM = 128
Q_LORA_RANK = 384
KV_LORA_RANK = 256
D_MLA = MLA_HEADS * V_HEAD_DIM
ROPE_THETA = 10000.0
SOFTMAX_SCALE = (QK_NOPE_DIM + QK_ROPE_DIM) ** -0.5
LOG2_E = 1.4426950408889634

V7X_LANES = 128
V7X_SUBLANES = 8
V7X_MXU_DIM = 256
V7X_VMEM_LIMIT_BYTES = 56 * 1024 * 1024

QK_PAD = V7X_MXU_DIM
ROPE_HALF = QK_ROPE_DIM // 2
ROPE_TILE = V7X_LANES
DT_PAD = V7X_LANES
KR_PAD = V7X_LANES
FF_CHUNK = V7X_MXU_DIM
SSD_BLOCK = 256
CONV_HIST = V7X_SUBLANES
TOKEN_TILE = 512
ATTN_TQ = 512
ATTN_TK = 256
ATTN_HEADS_PER_STEP = 4
ATTN_LOOP_TILES = 2
ATTN_ONES_ROWS = 2 * V7X_SUBLANES
NEG_BIG = -1e30

_C_Z = 0
_C_XBC = _C_Z + D_SSM
_C_QA = _C_XBC + CONV_DIM
_C_KVA = _C_QA + Q_LORA_RANK
_C_DT = _C_KVA + KV_LORA_RANK
_C_KR = _C_DT + DT_PAD
_C_END = _C_KR + KR_PAD


def _dot(a, b):
    return jnp.dot(a, b, preferred_element_type=F32)


def _rms(x, w):
    return x * lax.rsqrt(jnp.mean(x * x, axis=-1, keepdims=True) + EPS) * w


def _ffn(xn, wi_ref, wo_ref):
    n = D_FF // FF_CHUNK
    cols = lambda c: slice(c * FF_CHUNK, (c + 1) * FF_CHUNK)
    up_cols = lambda c: slice(D_FF + c * FF_CHUNK, D_FF + (c + 1) * FF_CHUNK)
    up = lambda c: (_dot(xn, wi_ref[:, cols(c)]), _dot(xn, wi_ref[:, up_cols(c)]))
    acc = None
    gu = up(0)
    for c in range(n):
        gu_next = up(c + 1) if c + 1 < n else None
        g, u = gu
        a = (g * jax.nn.sigmoid(g) * u).astype(BF16)
        d = _dot(a, wo_ref[cols(c), :])
        acc = d if acc is None else acc + d
        gu = gu_next
    return acc


def _const_spec(shape, grid_rank, layer=None):
    index = (0,) * len(shape) if layer is None else (layer,) + (0,) * (len(shape) - 1)
    block = tuple(shape) if layer is None else (None,) + tuple(shape[1:])
    if grid_rank == 1:
        index_map = lambda i: index
    elif grid_rank == 2:
        index_map = lambda i, j: index
    else:
        index_map = lambda i, j, k: index
    return pl.BlockSpec(block, index_map, pipeline_mode=pl.Buffered(1))


def _param_specs(params, grid_rank, layer):
    return [_const_spec(a.shape, grid_rank, layer if a.ndim == 3 else None) for a in params]


def _row_spec(tm, width):
    return pl.BlockSpec((tm, width), lambda i: (i, 0))


def _rope_kernel(pos_ref, inv_ref, lo_ref, hi_ref, c_ref, slo_ref, shi_ref):
    ang = pos_ref[...].astype(F32) * inv_ref[...]
    sin = jnp.sin(ang)
    c_ref[...] = jnp.cos(ang)
    slo_ref[...] = sin * lo_ref[...]
    shi_ref[...] = sin * hi_ref[...]


def _rope_tables(positions):
    t = positions.size
    tm = min(t, 1024)
    groups = ROPE_TILE // QK_ROPE_DIM
    inv = ROPE_THETA ** (-jnp.arange(0, QK_ROPE_DIM, 2, dtype=F32) / QK_ROPE_DIM)
    ones_h = jnp.ones((ROPE_HALF,), F32)
    zeros_h = jnp.zeros((ROPE_HALF,), F32)
    inv_row = jnp.tile(inv, 2 * groups)[None, :]
    lo = jnp.tile(jnp.concatenate([-ones_h, zeros_h]), groups)[None, :]
    hi = jnp.tile(jnp.concatenate([zeros_h, ones_h]), groups)[None, :]
    row = _const_spec((1, ROPE_TILE), 1)
    return pl.pallas_call(
        _rope_kernel,
        grid=(t // tm,),
        in_specs=[pl.BlockSpec((tm, 1), lambda i: (i, 0)), row, row, row],
        out_specs=[_row_spec(tm, ROPE_TILE)] * 3,
        out_shape=[jax.ShapeDtypeStruct((t, ROPE_TILE), F32)] * 3,
        compiler_params=pltpu.CompilerParams(dimension_semantics=("parallel",)),
        name="rope_tables",
    )(positions.reshape(t, 1), inv_row, lo, hi)


def _rotate(x, c, s_lo, s_hi):
    ahead = pltpu.roll(x, shift=ROPE_TILE - ROPE_HALF, axis=1)
    behind = pltpu.roll(x, shift=ROPE_HALF, axis=1)
    return x * c + ahead * s_lo + behind * s_hi


def _silu(x):
    return x * jax.nn.sigmoid(x)


def _pre_kernel(x_ref, n1_ref, wi_ref, wo_ref, nm_ref, win_ref, qn_ref, kn_ref,
                wqn_ref, wqr_ref, wk_ref, wv_ref, cw_ref, cb_ref, c_ref, slo_ref, shi_ref,
                h_ref, zg_ref, xbc_ref, dt_ref, q_ref, k_ref, v_ref, raw_ref, hist_ref,
                *, tiles_per_seq):
    i = pl.program_id(0)

    @pl.when(i == 0)
    def _():
        raw_ref[...] = jnp.zeros_like(raw_ref)
        hist_ref[...] = jnp.zeros_like(hist_ref)

    prev = raw_ref[...]
    seq_start = lax.rem(i - 1, tiles_per_seq) == 0
    hist = jnp.where(seq_start, 0.0, hist_ref[...])
    xpad = jnp.concatenate([hist, prev], axis=0)
    conv = cb_ref[...] + cw_ref[CONV_WIDTH - 1:CONV_WIDTH, :] * prev
    for back in range(1, CONV_WIDTH):
        shifted = pltpu.roll(xpad, shift=back, axis=0)[CONV_HIST:, :]
        conv = conv + cw_ref[CONV_WIDTH - 1 - back:CONV_WIDTH - back, :] * shifted
    xbc_ref[...] = _silu(conv)
    hist_ref[...] = prev[prev.shape[0] - CONV_HIST:, :]

    x = x_ref[...]
    xn = _rms(x, n1_ref[...]).astype(BF16)
    h1 = x + 0.5 * _ffn(xn, wi_ref, wo_ref)
    h_ref[...] = h1
    u = _rms(h1, nm_ref[...]).astype(BF16)
    tail = _dot(u, win_ref[:, _C_QA:_C_END])
    dt_ref[...] = tail[:, _C_DT - _C_QA:_C_KR - _C_QA]
    qn = _rms(tail[:, :_C_KVA - _C_QA], qn_ref[...]).astype(BF16)
    kvn = _rms(tail[:, _C_KVA - _C_QA:_C_DT - _C_QA], kn_ref[...]).astype(BF16)
    c, s_lo, s_hi = c_ref[...], slo_ref[...], shi_ref[...]
    lane = lax.broadcasted_iota(jnp.int32, (1, ROPE_TILE), 1)
    first_group = lane < QK_ROPE_DIM
    kr = _rotate(tail[:, _C_KR - _C_QA:], c, s_lo, s_hi).astype(BF16)
    q_scale = SOFTMAX_SCALE * LOG2_E
    q_nope = _dot(qn, wqn_ref[...]) * q_scale
    k_nope = _dot(kvn, wk_ref[...])
    q_rope = _dot(qn, wqr_ref[...])
    for pair in range(MLA_HEADS // 2):
        tile = slice(pair * ROPE_TILE, (pair + 1) * ROPE_TILE)
        rot = _rotate(q_rope[:, tile], c, s_lo, s_hi) * q_scale
        for sub, part in enumerate((jnp.where(first_group, rot, 0.0),
                                    jnp.where(first_group, pltpu.roll(rot, shift=QK_ROPE_DIM, axis=1), 0.0))):
            base = (2 * pair + sub) * QK_PAD
            q_ref[:, base + QK_NOPE_DIM:base + QK_PAD] = part.astype(BF16)
    for h in range(MLA_HEADS):
        nope = slice(h * QK_NOPE_DIM, (h + 1) * QK_NOPE_DIM)
        q_ref[:, h * QK_PAD:h * QK_PAD + QK_NOPE_DIM] = q_nope[:, nope].astype(BF16)
        k_ref[:, h * QK_PAD:h * QK_PAD + QK_NOPE_DIM] = k_nope[:, nope].astype(BF16)
        k_ref[:, h * QK_PAD + QK_NOPE_DIM:(h + 1) * QK_PAD] = kr
    v_ref[0] = lax.dot_general(wv_ref[...], kvn, (((1,), (1,)), ((), ())),
                               preferred_element_type=F32).astype(BF16)
    zg_ref[...] = _silu(_dot(u, win_ref[:, _C_Z:_C_XBC]))
    raw_ref[...] = _dot(u, win_ref[:, _C_XBC:_C_QA])


def _pre_call(h, params, rope, seq, layer):
    t = h.shape[0]
    tm = ATTN_TK
    n = t // tm
    cur = lambda i: (jnp.minimum(i, n - 1), 0)
    prev = lambda i: (jnp.maximum(i - 1, 0), 0)
    in_specs = ([pl.BlockSpec((tm, D_MODEL), cur)] + _param_specs(params, 1, layer)
                + [pl.BlockSpec((tm, ROPE_TILE), cur)] * len(rope))
    outs = [(D_MODEL, F32, cur), (D_SSM, F32, cur), (CONV_DIM, F32, prev), (DT_PAD, F32, cur),
            (MLA_HEADS * QK_PAD, BF16, cur), (MLA_HEADS * QK_PAD, BF16, cur)]
    return pl.pallas_call(
        functools.partial(_pre_kernel, tiles_per_seq=seq // tm),
        grid=(n + 1,),
        in_specs=in_specs,
        out_specs=[pl.BlockSpec((tm, w), im) for w, _, im in outs]
        + [pl.BlockSpec((1, D_MLA, tm), lambda i: (jnp.minimum(i, n - 1), 0, 0))],
        out_shape=[jax.ShapeDtypeStruct((t, w), d) for w, d, _ in outs]
        + [jax.ShapeDtypeStruct((n, D_MLA, tm), BF16)],
        scratch_shapes=[pltpu.VMEM((tm, CONV_DIM), F32), pltpu.VMEM((CONV_HIST, CONV_DIM), F32)],
        compiler_params=pltpu.CompilerParams(
            dimension_semantics=("arbitrary",), vmem_limit_bytes=V7X_VMEM_LIMIT_BYTES),
        name="pre",
    )(h, *params, *rope)


def _split_dot(v, m, parts, left=False):
    out = None
    rem = v
    for _ in range(parts):
        piece = rem.astype(BF16)
        rem = rem - piece.astype(F32)
        d = _dot(m, piece) if left else _dot(piece, m)
        out = d if out is None else out + d
    return out


def _softplus(x):
    return jnp.maximum(x, 0.0) + jnp.log1p(jnp.exp(-jnp.abs(x)))


def _ssd_kernel(xbc_ref, zg_ref, dt_ref, dtb_ref, alog_ref, dskip_ref, nw_ref,
                tri_ref, exp_ref, y_ref, state_ref):
    q = SSD_BLOCK
    hp = SSD_HEADS_PER_GROUP * SSD_HEAD_DIM

    @pl.when(pl.program_id(1) == 0)
    def _():
        state_ref[...] = jnp.zeros_like(state_ref)

    xbc = xbc_ref[...]
    xs = xbc[:, :D_SSM]

    lane = lax.broadcasted_iota(jnp.int32, (1, DT_PAD), 1)
    head_lane = lane < SSD_HEADS
    dt = jnp.where(head_lane, _softplus(dt_ref[...] + dtb_ref[...]), 0.0)
    a_neg = jnp.where(head_lane, -jnp.exp(alog_ref[...]), 0.0)
    a_cum = _split_dot(dt * (a_neg * LOG2_E), tri_ref[...], 3, left=True)
    a_cum_t = a_cum.T
    a_last = a_cum[q - 1:q, :]
    to_end = jnp.exp2(a_last - a_cum)
    from_start = jnp.exp2(a_cum)

    def expand(v):
        hi = v.astype(BF16)
        lo = (v - hi.astype(F32)).astype(BF16)
        return _dot(jnp.concatenate([hi, lo], axis=1), exp_ref[...])

    dt_e = expand(dt)
    w_e = expand(dt * to_end)
    fs_e = expand(from_start)
    xdt = (xs * dt_e).astype(BF16)
    xw = (xs * w_e).astype(BF16)

    row = lax.broadcasted_iota(jnp.int32, (q, q), 0)
    col = lax.broadcasted_iota(jnp.int32, (q, q), 1)
    causal = col <= row
    lane_pair = lax.broadcasted_iota(jnp.int32, (1, 2 * SSD_HEAD_DIM), 1)
    first_of_pair = lane_pair < SSD_HEAD_DIM

    zg = zg_ref[...]
    for g in range(SSD_GROUPS):
        b_g = xbc[:, D_SSM + g * SSD_STATE:D_SSM + (g + 1) * SSD_STATE]
        c_g = xbc[:, D_SSM + (SSD_GROUPS + g) * SSD_STATE:D_SSM + (SSD_GROUPS + g + 1) * SSD_STATE]
        b_bf = b_g.astype(BF16)
        c_bf = c_g.astype(BF16)
        cb = lax.dot_general(c_bf, b_bf, (((1,), (1,)), ((), ())), preferred_element_type=F32)
        gsl = slice(g * hp, (g + 1) * hp)
        state = state_ref[g]
        y_g = _dot(c_bf, state.astype(BF16)) * fs_e[:, gsl] + xs[:, gsl] * dskip_ref[:, gsl]
        pieces = []
        for pair in range(SSD_HEADS_PER_GROUP // 2):
            outs = []
            for sub in range(2):
                h = g * SSD_HEADS_PER_GROUP + 2 * pair + sub
                seg = a_cum[:, h:h + 1] - a_cum_t[h:h + 1, :]
                m = (cb * jnp.exp2(jnp.where(causal, seg, NEG_BIG))).astype(BF16)
                psl = slice(g * hp + pair * 2 * SSD_HEAD_DIM, g * hp + (pair + 1) * 2 * SSD_HEAD_DIM)
                outs.append(_dot(m, xdt[:, psl]))
            pieces.append(jnp.where(first_of_pair, outs[0], outs[1]))
        y_g = y_g + jnp.concatenate(pieces, axis=-1)
        y_g = y_g * zg[:, gsl]
        y_g = y_g * lax.rsqrt(jnp.mean(y_g * y_g, axis=-1, keepdims=True) + EPS) * nw_ref[:, gsl]
        y_ref[:, gsl] = y_g.astype(y_ref.dtype)
        state_ref[g] = state * fs_e[q - 1:q, gsl] + _dot(b_g.T.astype(BF16), xw[:, gsl])


def _ssd_call(xbc, zg, dtm, params, batch, seq, layer):
    q = SSD_BLOCK
    nblk = seq // q
    hp = SSD_HEADS_PER_GROUP * SSD_HEAD_DIM
    blk = lambda w: pl.BlockSpec((q, w), lambda b, c: (b * nblk + c, 0))
    return pl.pallas_call(
        _ssd_kernel,
        grid=(batch, nblk),
        in_specs=[blk(CONV_DIM), blk(D_SSM), blk(DT_PAD)] + _param_specs(params, 2, layer),
        out_specs=blk(D_SSM),
        out_shape=jax.ShapeDtypeStruct((batch * seq, D_SSM), BF16),
        scratch_shapes=[pltpu.VMEM((SSD_GROUPS, SSD_STATE, hp), F32)],
        compiler_params=pltpu.CompilerParams(
            dimension_semantics=("arbitrary", "arbitrary"), vmem_limit_bytes=V7X_VMEM_LIMIT_BYTES),
        name="ssd",
    )(xbc, zg, dtm, *params)


def _attn_kernel(q_ref, k_ref, vt_ref, o_ref, m_ref, l_ref, acc_ref):
    tq, tk = q_ref.shape[0], ATTN_TK
    qi = pl.program_id(2)
    heads = range(ATTN_HEADS_PER_STEP)
    ones = jnp.ones((ATTN_ONES_ROWS, tk), BF16)

    def qk(g, j, tiles, q_lo):
        start = pl.multiple_of(j * tk, tk)
        k = k_ref[pl.ds(start, tiles * tk), g * QK_PAD:(g + 1) * QK_PAD]
        q = q_ref[q_lo:, g * QK_PAD:(g + 1) * QK_PAD]
        return lax.dot_general(k, q, (((1,), (1,)), ((), ())), preferred_element_type=F32)

    def fold(g, s, j, tiles, q_lo, mask, first):
        if mask is not None:
            s = jnp.where(mask, s, NEG_BIG)
        m_blk = jnp.max(s, axis=0, keepdims=True)
        if first:
            m_new = m_blk
        else:
            m_old = m_ref[g, :, q_lo:]
            m_new = jnp.maximum(m_old, m_blk)
            alpha = jnp.exp2(m_old - m_new)
        p = jnp.exp2(s - m_new).astype(BF16)
        pv = None
        for i in range(tiles):
            vt = jnp.concatenate([vt_ref[j + i, g * V_HEAD_DIM:(g + 1) * V_HEAD_DIM, :], ones], axis=0)
            d = _dot(vt, p[i * tk:(i + 1) * tk, :])
            pv = d if pv is None else pv + d
        p_sum = pv[V_HEAD_DIM:V_HEAD_DIM + 1, :]
        pv = pv[:V_HEAD_DIM, :]
        if first:
            l_ref[g, :, q_lo:] = p_sum
            acc_ref[g, :, q_lo:] = pv
        else:
            l_ref[g, :, q_lo:] = alpha * l_ref[g, :, q_lo:] + p_sum
            acc_ref[g, :, q_lo:] = alpha * acc_ref[g, :, q_lo:] + pv
        m_ref[g, :, q_lo:] = m_new

    def sweep(j, tiles, q_lo, mask, first):
        scores = [qk(g, j, tiles, q_lo) for g in heads]
        for g in heads:
            fold(g, scores[g], j, tiles, q_lo, mask, first)

    for d in range(tq // tk):
        q_lo = d * tk
        key = lax.broadcasted_iota(jnp.int32, (tk, tq - q_lo), 0)
        query = lax.broadcasted_iota(jnp.int32, (tk, tq - q_lo), 1)
        sweep(qi * (tq // tk) + d, 1, q_lo, key // CHUNK <= query // CHUNK, first=(d == 0))

    @pl.loop(0, qi * (tq // tk) // ATTN_LOOP_TILES)
    def _(trip):
        sweep(trip * ATTN_LOOP_TILES, ATTN_LOOP_TILES, 0, None, first=False)

    for g in heads:
        o_ref[:, g * V_HEAD_DIM:(g + 1) * V_HEAD_DIM] = (acc_ref[g] / l_ref[g]).T.astype(o_ref.dtype)


def _attn_call(q, k, vt, batch, seq):
    tq = min(seq, ATTN_TQ)
    nq = seq // tq
    g = ATTN_HEADS_PER_STEP
    assert (tq // ATTN_TK) % ATTN_LOOP_TILES == 0 and seq % tq == 0 and MLA_HEADS % g == 0
    return pl.pallas_call(
        _attn_kernel,
        grid=(batch, MLA_HEADS // g, nq),
        in_specs=[pl.BlockSpec((tq, g * QK_PAD), lambda b, h, i: (b * nq + i, h)),
                  pl.BlockSpec((seq, g * QK_PAD), lambda b, h, i: (b, h)),
                  pl.BlockSpec((seq // ATTN_TK, g * V_HEAD_DIM, ATTN_TK), lambda b, h, i: (b, h, 0))],
        out_specs=pl.BlockSpec((tq, g * V_HEAD_DIM), lambda b, h, i: (b * nq + i, h)),
        out_shape=jax.ShapeDtypeStruct((batch * seq, D_MLA), BF16),
        scratch_shapes=[pltpu.VMEM((g, 1, tq), F32), pltpu.VMEM((g, 1, tq), F32),
                        pltpu.VMEM((g, V_HEAD_DIM, tq), F32)],
        compiler_params=pltpu.CompilerParams(
            dimension_semantics=("parallel", "parallel", "arbitrary"),
            vmem_limit_bytes=V7X_VMEM_LIMIT_BYTES),
        name="attn",
    )(q, k, vt)


def _post_kernel(h_ref, ys_ref, ym_ref, p_ref, wmix_ref, n2_ref, wi_ref, wo_ref,
                 np_ref, wgate_ref, wproj_ref, nf_ref, o_ref, *, final):
    h2 = (h_ref[...] + _dot(ys_ref[...], wmix_ref[:D_SSM, :]) + _dot(ym_ref[...], wmix_ref[D_SSM:, :]))
    xn = _rms(h2, n2_ref[...]).astype(BF16)
    h3 = h2 + 0.5 * _ffn(xn, wi_ref, wo_ref)
    gate = jax.nn.sigmoid(_dot(_rms(h3, np_ref[...]).astype(BF16), wgate_ref[...]))
    h4 = h3 + gate * _dot(p_ref[...].astype(BF16), wproj_ref[...])
    if final:
        h4 = _rms(h4, nf_ref[...])
    o_ref[...] = h4


def _post_call(h, ys, ym, p, params, layer, final):
    t = h.shape[0]
    tm = min(t, TOKEN_TILE)
    in_specs = ([_row_spec(tm, D_MODEL), _row_spec(tm, D_SSM), _row_spec(tm, D_MLA),
                 pl.BlockSpec((None, tm, PLE_DIM), lambda i: (layer, i, 0))]
                + _param_specs(params, 1, layer))
    return pl.pallas_call(
        functools.partial(_post_kernel, final=final),
        grid=(t // tm,),
        in_specs=in_specs,
        out_specs=_row_spec(tm, D_MODEL),
        out_shape=jax.ShapeDtypeStruct((t, D_MODEL), F32),
        compiler_params=pltpu.CompilerParams(
            dimension_semantics=("parallel",), vmem_limit_bytes=V7X_VMEM_LIMIT_BYTES),
        name="post",
    )(h, ys, ym, p, *params)


def _pad_last(w, width):
    return jnp.pad(w, [(0, 0)] * (w.ndim - 1) + [(0, width - w.shape[-1])])


def _layout_in_proj(w):
    o = 0
    w_z = w[..., o:o + D_SSM]; o += D_SSM
    w_xbc = w[..., o:o + CONV_DIM]; o += CONV_DIM
    w_dt = w[..., o:o + SSD_HEADS]; o += SSD_HEADS
    w_qa = w[..., o:o + Q_LORA_RANK]; o += Q_LORA_RANK
    w_kva = w[..., o:o + KV_LORA_RANK]; o += KV_LORA_RANK
    w_kr = w[..., o:o + QK_ROPE_DIM]
    return jnp.concatenate([
        w_z, w_xbc, w_qa, w_kva, _pad_last(w_dt, DT_PAD), _pad_last(w_kr, KR_PAD)], axis=-1).astype(BF16)


def _layout_q_b(w):
    depth = w.shape[0]
    w = w.reshape(depth, Q_LORA_RANK, MLA_HEADS, QK_NOPE_DIM + QK_ROPE_DIM)
    return (w[..., :QK_NOPE_DIM].reshape(depth, Q_LORA_RANK, MLA_HEADS * QK_NOPE_DIM).astype(BF16),
            w[..., QK_NOPE_DIM:].reshape(depth, Q_LORA_RANK, MLA_HEADS * QK_ROPE_DIM).astype(BF16))


def _layout_kv_b(w):
    depth = w.shape[0]
    w = w.reshape(depth, KV_LORA_RANK, MLA_HEADS, QK_NOPE_DIM + V_HEAD_DIM)
    wk = w[..., :QK_NOPE_DIM].reshape(depth, KV_LORA_RANK, MLA_HEADS * QK_NOPE_DIM)
    wv_t = jnp.swapaxes(w[..., QK_NOPE_DIM:].reshape(depth, KV_LORA_RANK, D_MLA), 1, 2)
    return wk.astype(BF16), wv_t.astype(BF16)


def _rows(v):
    return v.reshape(v.shape[0], 1, -1).astype(F32)


def kernel(x, p, positions, ffn1_norm, ffn1_w_in, ffn1_w_out, mix_norm, w_in_mix, conv_w, conv_b,
           dt_bias, a_log, d_skip, ssd_norm, q_a_norm, w_q_b, kv_a_norm, w_kv_b, w_out_mix,
           ffn2_norm, ffn2_w_in, ffn2_w_out, ple_norm, w_ple_gate, w_ple_proj, final_norm):
    batch, seq, _ = x.shape
    depth = p.shape[0]
    t = batch * seq
    assert seq % SSD_BLOCK == 0 and seq % min(seq, ATTN_TQ) == 0 and t % TOKEN_TILE == 0

    rope = _rope_tables(positions)
    q_idx = jnp.arange(SSD_BLOCK)
    tri = (q_idx[None, :] <= q_idx[:, None]).astype(BF16)
    expand = (jnp.arange(D_SSM)[None, :] // SSD_HEAD_DIM == jnp.arange(DT_PAD)[:, None]).astype(BF16)
    expand = jnp.concatenate([expand, expand], axis=0)

    wqn, wqr = _layout_q_b(w_q_b)
    wk, wv = _layout_kv_b(w_kv_b)
    pre_params = [_rows(ffn1_norm), ffn1_w_in.astype(BF16), ffn1_w_out.astype(BF16), _rows(mix_norm),
                  _layout_in_proj(w_in_mix), _rows(q_a_norm), _rows(kv_a_norm), wqn, wqr, wk, wv,
                  conv_w.astype(F32), _rows(conv_b)]
    ssd_params = [_pad_last(_rows(dt_bias), DT_PAD), _pad_last(_rows(a_log), DT_PAD),
                  _rows(jnp.repeat(d_skip, SSD_HEAD_DIM, axis=-1)), _rows(ssd_norm), tri, expand]
    post_params = [w_out_mix.astype(BF16), _rows(ffn2_norm), ffn2_w_in.astype(BF16),
                   ffn2_w_out.astype(BF16), _rows(ple_norm), w_ple_gate.astype(BF16),
                   w_ple_proj.astype(BF16), final_norm.reshape(1, -1).astype(F32)]
    p = p.reshape(depth, t, PLE_DIM)

    h = x.reshape(t, D_MODEL)
    for i in range(depth):
        h, zg, xbc, dtm, q, k, v = _pre_call(h, pre_params, rope, seq, i)
        y_ssd = _ssd_call(xbc, zg, dtm, ssd_params, batch, seq, i)
        y_mla = _attn_call(q, k, v, batch, seq)
        h = _post_call(h, y_ssd, y_mla, p, post_params, i, final=(i == depth - 1))
    return h.reshape(batch, seq, D_MODEL)
```

```python
import functools

import jax
import jax.numpy as jnp
from jax import lax
from jax.experimental import pallas as pl
from jax.experimental.pallas import tpu as pltpu

F32 = jnp.float32
BF16 = jnp.bfloat16

D_MODEL = 1024
CHUNK = 64
PLE_DIM = 256
D_FF = 2816
EPS = 1e-6
SSD_HEADS = 16
SSD_HEAD_DIM = 64
D_SSM = SSD_HEADS * SSD_HEAD_DIM
SSD_GROUPS = 2
SSD_HEADS_PER_GROUP = SSD_HEADS // SSD_GROUPS
SSD_STATE = 128
CONV_WIDTH = 4
CONV_DIM = D_SSM + 2 * SSD_GROUPS * SSD_STATE
MLA_HEADS = 8
QK_NOPE_DIM = 128
QK_ROPE_DIM = 64
V_HEAD_DIM = 128
Q_LORA_RANK = 384
KV_LORA_RANK = 256
D_MLA = MLA_HEADS * V_HEAD_DIM
ROPE_THETA = 10000.0
SOFTMAX_SCALE = (QK_NOPE_DIM + QK_ROPE_DIM) ** -0.5
LOG2_E = 1.4426950408889634

V7X_LANES = 128
V7X_SUBLANES = 8
V7X_MXU_DIM = 256
V7X_VMEM_LIMIT_BYTES = 56 * 1024 * 1024

QK_PAD = V7X_MXU_DIM
ROPE_HALF = QK_ROPE_DIM // 2
ROPE_TILE = V7X_LANES
DT_PAD = V7X_LANES
FF_CHUNK = V7X_MXU_DIM
SSD_BLOCK = 256
CONV_HIST = V7X_SUBLANES
TOKEN_TILE = 512
ATTN_TQ = 512
ATTN_TK = 256
ATTN_HEADS_PER_STEP = 4
ATTN_LOOP_TILES = 2
ATTN_ONES_ROWS = 2 * V7X_SUBLANES
NEG_BIG = -1e30

_C_Z = 0
_C_XBC = _C_Z + D_SSM
_C_QA = _C_XBC + CONV_DIM
_C_KVA = _C_QA + Q_LORA_RANK
_C_DT = _C_KVA + KV_LORA_RANK
_C_END = _C_DT + DT_PAD


def _dot(a, b):
    return jnp.dot(a, b, preferred_element_type=F32)


def _rms(x, w):
    return x * lax.rsqrt(jnp.mean(x * x, axis=-1, keepdims=True) + EPS) * w


def _ffn(xn, wi_ref, wo_ref):
    n = D_FF // FF_CHUNK
    cols = lambda c: slice(c * FF_CHUNK, (c + 1) * FF_CHUNK)
    up_cols = lambda c: slice(D_FF + c * FF_CHUNK, D_FF + (c + 1) * FF_CHUNK)
    up = lambda c: (_dot(xn, wi_ref[:, cols(c)]), _dot(xn, wi_ref[:, up_cols(c)]))
    acc = None
    gu = up(0)
    for c in range(n):
        gu_next = up(c + 1) if c + 1 < n else None
        g, u = gu
        a = (g * jax.nn.sigmoid(g) * u).astype(BF16)
        d = _dot(a, wo_ref[cols(c), :])
        acc = d if acc is None else acc + d
        gu = gu_next
    return acc


def _const_spec(shape, grid_rank, layer=None):
    index = (0,) * len(shape) if layer is None else (layer,) + (0,) * (len(shape) - 1)
    block = tuple(shape) if layer is None else (None,) + tuple(shape[1:])
    if grid_rank == 1:
        index_map = lambda i: index
    elif grid_rank == 2:
        index_map = lambda i, j: index
    else:
        index_map = lambda i, j, k: index
    return pl.BlockSpec(block, index_map, pipeline_mode=pl.Buffered(1))


def _param_specs(params, grid_rank, layer):
    return [_const_spec(a.shape, grid_rank, layer if a.ndim == 3 else None) for a in params]


def _row_spec(tm, width):
    return pl.BlockSpec((tm, width), lambda i: (i, 0))


def _rope_kernel(pos_ref, inv_ref, lo_ref, hi_ref, c_ref, slo_ref, shi_ref):
    ang = pos_ref[...].astype(F32) * inv_ref[...]
    sin = jnp.sin(ang)
    c_ref[...] = jnp.cos(ang)
    slo_ref[...] = sin * lo_ref[...]
    shi_ref[...] = sin * hi_ref[...]


def _rope_tables(positions):
    t = positions.size
    tm = min(t, 1024)
    groups = ROPE_TILE // QK_ROPE_DIM
    inv = ROPE_THETA ** (-jnp.arange(0, QK_ROPE_DIM, 2, dtype=F32) / QK_ROPE_DIM)
    ones_h = jnp.ones((ROPE_HALF,), F32)
    zeros_h = jnp.zeros((ROPE_HALF,), F32)
    inv_row = jnp.tile(inv, 2 * groups)[None, :]
    lo = jnp.tile(jnp.concatenate([-ones_h, zeros_h]), groups)[None, :]
    hi = jnp.tile(jnp.concatenate([zeros_h, ones_h]), groups)[None, :]
    row = _const_spec((1, ROPE_TILE), 1)
    return pl.pallas_call(
        _rope_kernel,
        grid=(t // tm,),
        in_specs=[pl.BlockSpec((tm, 1), lambda i: (i, 0)), row, row, row],
        out_specs=[_row_spec(tm, ROPE_TILE)] * 3,
        out_shape=[jax.ShapeDtypeStruct((t, ROPE_TILE), F32)] * 3,
        compiler_params=pltpu.CompilerParams(dimension_semantics=("parallel",)),
        name="rope_tables",
    )(positions.reshape(t, 1), inv_row, lo, hi)


def _rotate(x, c, s_lo, s_hi):
    ahead = pltpu.roll(x, shift=ROPE_TILE - ROPE_HALF, axis=1)
    behind = pltpu.roll(x, shift=ROPE_HALF, axis=1)
    return x * c + ahead * s_lo + behind * s_hi


def _silu(x):
    return x * jax.nn.sigmoid(x)


def _pre_kernel(x_ref, n1_ref, wi_ref, wo_ref, nm_ref, win_ref, qn_ref, kn_ref,
                wqn_ref, wqr_ref, wk_ref, wv_ref, cw_ref, cb_ref, c_ref, slo_ref, shi_ref,
                h_ref, zg_ref, xbc_ref, dt_ref, q_ref, k_ref, v_ref, raw_ref, hist_ref,
                *, tiles_per_seq):
    i = pl.program_id(0)

    @pl.when(i == 0)
    def _():
        raw_ref[...] = jnp.zeros_like(raw_ref)
        hist_ref[...] = jnp.zeros_like(hist_ref)

    prev = raw_ref[...]
    seq_start = lax.rem(i - 1, tiles_per_seq) == 0
    hist = jnp.where(seq_start, 0.0, hist_ref[...])
    xpad = jnp.concatenate([hist, prev], axis=0)
    conv = cb_ref[...] + cw_ref[CONV_WIDTH - 1:CONV_WIDTH, :] * prev
    for back in range(1, CONV_WIDTH):
        shifted = pltpu.roll(xpad, shift=back, axis=0)[CONV_HIST:, :]
        conv = conv + cw_ref[CONV_WIDTH - 1 - back:CONV_WIDTH - back, :] * shifted
    xbc_ref[...] = _silu(conv)
    hist_ref[...] = prev[prev.shape[0] - CONV_HIST:, :]

    x = x_ref[...]
    xn = _rms(x, n1_ref[...]).astype(BF16)
    h1 = x + 0.5 * _ffn(xn, wi_ref, wo_ref)
    h_ref[...] = h1
    u = _rms(h1, nm_ref[...]).astype(BF16)
    tail = _dot(u, win_ref[:, _C_QA:_C_END])
    misc = tail[:, _C_DT - _C_QA:]
    dt_ref[...] = misc
    qn = _rms(tail[:, :_C_KVA - _C_QA], qn_ref[...]).astype(BF16)
    kvn = _rms(tail[:, _C_KVA - _C_QA:_C_DT - _C_QA], kn_ref[...]).astype(BF16)
    c, s_lo, s_hi = c_ref[...], slo_ref[...], shi_ref[...]
    lane = lax.broadcasted_iota(jnp.int32, (1, ROPE_TILE), 1)
    first_group = lane < QK_ROPE_DIM
    kr = jnp.where(first_group, pltpu.roll(_rotate(misc, c, s_lo, s_hi), shift=QK_ROPE_DIM, axis=1),
                   0.0).astype(BF16)
    q_scale = SOFTMAX_SCALE * LOG2_E
    q_nope = _dot(qn, wqn_ref[...]) * q_scale
    k_nope = _dot(kvn, wk_ref[...])
    q_rope = _dot(qn, wqr_ref[...])
    for pair in range(MLA_HEADS // 2):
        tile = slice(pair * ROPE_TILE, (pair + 1) * ROPE_TILE)
        rot = _rotate(q_rope[:, tile], c, s_lo, s_hi) * q_scale
        for sub, part in enumerate((jnp.where(first_group, rot, 0.0),
                                    jnp.where(first_group, pltpu.roll(rot, shift=QK_ROPE_DIM, axis=1), 0.0))):
            base = (2 * pair + sub) * QK_PAD
            q_ref[:, base + QK_NOPE_DIM:base + QK_PAD] = part.astype(BF16)
    for h in range(MLA_HEADS):
        nope = slice(h * QK_NOPE_DIM, (h + 1) * QK_NOPE_DIM)
        q_ref[:, h * QK_PAD:h * QK_PAD + QK_NOPE_DIM] = q_nope[:, nope].astype(BF16)
        k_ref[:, h * QK_PAD:h * QK_PAD + QK_NOPE_DIM] = k_nope[:, nope].astype(BF16)
        k_ref[:, h * QK_PAD + QK_NOPE_DIM:(h + 1) * QK_PAD] = kr
    v_ref[0] = lax.dot_general(wv_ref[...], kvn, (((1,), (1,)), ((), ())),
                               preferred_element_type=F32).astype(BF16)
    zg_ref[...] = _silu(_dot(u, win_ref[:, _C_Z:_C_XBC]))
    raw_ref[...] = _dot(u, win_ref[:, _C_XBC:_C_QA])


def _pre_call(h, params, rope, seq, layer):
    t = h.shape[0]
    tm = ATTN_TK
    n = t // tm
    cur = lambda i: (jnp.minimum(i, n - 1), 0)
    prev = lambda i: (jnp.maximum(i - 1, 0), 0)
    in_specs = ([pl.BlockSpec((tm, D_MODEL), cur)] + _param_specs(params, 1, layer)
                + [pl.BlockSpec((tm, ROPE_TILE), cur)] * len(rope))
    outs = [(D_MODEL, F32, cur), (D_SSM, F32, cur), (CONV_DIM, F32, prev), (DT_PAD, F32, cur),
            (MLA_HEADS * QK_PAD, BF16, cur), (MLA_HEADS * QK_PAD, BF16, cur)]
    return pl.pallas_call(
        functools.partial(_pre_kernel, tiles_per_seq=seq // tm),
        grid=(n + 1,),
        in_specs=in_specs,
        out_specs=[pl.BlockSpec((tm, w), im) for w, _, im in outs]
        + [pl.BlockSpec((1, D_MLA, tm), lambda i: (jnp.minimum(i, n - 1), 0, 0))],
        out_shape=[jax.ShapeDtypeStruct((t, w), d) for w, d, _ in outs]
        + [jax.ShapeDtypeStruct((n, D_MLA, tm), BF16)],
        scratch_shapes=[pltpu.VMEM((tm, CONV_DIM), F32), pltpu.VMEM((CONV_HIST, CONV_DIM), F32)],
        compiler_params=pltpu.CompilerParams(
            dimension_semantics=("arbitrary",), vmem_limit_bytes=V7X_VMEM_LIMIT_BYTES),
        name="pre",
    )(h, *params, *rope)


def _split_dot(v, m, parts, left=False):
    out = None
    rem = v
    for _ in range(parts):
        piece = rem.astype(BF16)
        rem = rem - piece.astype(F32)
        d = _dot(m, piece) if left else _dot(piece, m)
        out = d if out is None else out + d
    return out


def _softplus(x):
    return jnp.maximum(x, 0.0) + jnp.log1p(jnp.exp(-jnp.abs(x)))


def _ssd_kernel(xbc_ref, zg_ref, dt_ref, dtb_ref, alog_ref, dskip_ref, nw_ref,
                tri_ref, exp_ref, y_ref, state_ref):
    q = SSD_BLOCK
    hp = SSD_HEADS_PER_GROUP * SSD_HEAD_DIM

    @pl.when(pl.program_id(1) == 0)
    def _():
        state_ref[...] = jnp.zeros_like(state_ref)

    xbc = xbc_ref[...]
    xs = xbc[:, :D_SSM]

    lane = lax.broadcasted_iota(jnp.int32, (1, DT_PAD), 1)
    head_lane = lane < SSD_HEADS
    dt = jnp.where(head_lane, _softplus(dt_ref[...] + dtb_ref[...]), 0.0)
    a_neg = jnp.where(head_lane, -jnp.exp(alog_ref[...]), 0.0)
    a_cum = _split_dot(dt * (a_neg * LOG2_E), tri_ref[...], 3, left=True)
    a_cum_t = a_cum.T
    a_last = a_cum[q - 1:q, :]
    to_end = jnp.exp2(a_last - a_cum)
    from_start = jnp.exp2(a_cum)

    def expand(v):
        hi = v.astype(BF16)
        lo = (v - hi.astype(F32)).astype(BF16)
        return _dot(jnp.concatenate([hi, lo], axis=1), exp_ref[...])

    dt_e = expand(dt)
    w_e = expand(dt * to_end)
    fs_e = expand(from_start)
    xdt = (xs * dt_e).astype(BF16)
    xw = (xs * w_e).astype(BF16)

    row = lax.broadcasted_iota(jnp.int32, (q, q), 0)
    col = lax.broadcasted_iota(jnp.int32, (q, q), 1)
    causal = col <= row
    lane_pair = lax.broadcasted_iota(jnp.int32, (1, 2 * SSD_HEAD_DIM), 1)
    first_of_pair = lane_pair < SSD_HEAD_DIM

    zg = zg_ref[...]
    for g in range(SSD_GROUPS):
        b_g = xbc[:, D_SSM + g * SSD_STATE:D_SSM + (g + 1) * SSD_STATE]
        c_g = xbc[:, D_SSM + (SSD_GROUPS + g) * SSD_STATE:D_SSM + (SSD_GROUPS + g + 1) * SSD_STATE]
        b_bf = b_g.astype(BF16)
        c_bf = c_g.astype(BF16)
        cb = lax.dot_general(c_bf, b_bf, (((1,), (1,)), ((), ())), preferred_element_type=F32)
        gsl = slice(g * hp, (g + 1) * hp)
        state = state_ref[g]
        y_g = _dot(c_bf, state.astype(BF16)) * fs_e[:, gsl] + xs[:, gsl] * dskip_ref[:, gsl]
        pieces = []
        for pair in range(SSD_HEADS_PER_GROUP // 2):
            outs = []
            for sub in range(2):
                h = g * SSD_HEADS_PER_GROUP + 2 * pair + sub
                seg = a_cum[:, h:h + 1] - a_cum_t[h:h + 1, :]
                m = (cb * jnp.exp2(jnp.where(causal, seg, NEG_BIG))).astype(BF16)
                psl = slice(g * hp + pair * 2 * SSD_HEAD_DIM, g * hp + (pair + 1) * 2 * SSD_HEAD_DIM)
                outs.append(_dot(m, xdt[:, psl]))
            pieces.append(jnp.where(first_of_pair, outs[0], outs[1]))
        y_g = y_g + jnp.concatenate(pieces, axis=-1)
        y_g = y_g * zg[:, gsl]
        y_g = y_g * lax.rsqrt(jnp.mean(y_g * y_g, axis=-1, keepdims=True) + EPS) * nw_ref[:, gsl]
        y_ref[:, gsl] = y_g.astype(y_ref.dtype)
        state_ref[g] = state * fs_e[q - 1:q, gsl] + _dot(b_g.T.astype(BF16), xw[:, gsl])


def _ssd_call(xbc, zg, dtm, params, batch, seq, layer):
    q = SSD_BLOCK
    nblk = seq // q
    hp = SSD_HEADS_PER_GROUP * SSD_HEAD_DIM
    blk = lambda w: pl.BlockSpec((q, w), lambda b, c: (b * nblk + c, 0))
    return pl.pallas_call(
        _ssd_kernel,
        grid=(batch, nblk),
        in_specs=[blk(CONV_DIM), blk(D_SSM), blk(DT_PAD)] + _param_specs(params, 2, layer),
        out_specs=blk(D_SSM),
        out_shape=jax.ShapeDtypeStruct((batch * seq, D_SSM), BF16),
        scratch_shapes=[pltpu.VMEM((SSD_GROUPS, SSD_STATE, hp), F32)],
        compiler_params=pltpu.CompilerParams(
            dimension_semantics=("arbitrary", "arbitrary"), vmem_limit_bytes=V7X_VMEM_LIMIT_BYTES),
        name="ssd",
    )(xbc, zg, dtm, *params)


def _attn_kernel(q_ref, k_ref, vt_ref, o_ref, m_ref, l_ref, acc_ref):
    tq, tk = q_ref.shape[0], ATTN_TK
    qi = pl.program_id(2)
    heads = range(ATTN_HEADS_PER_STEP)
    ones = jnp.ones((ATTN_ONES_ROWS, tk), BF16)

    def qk(g, j, tiles, q_lo):
        start = pl.multiple_of(j * tk, tk)
        k = k_ref[pl.ds(start, tiles * tk), g * QK_PAD:(g + 1) * QK_PAD]
        q = q_ref[q_lo:, g * QK_PAD:(g + 1) * QK_PAD]
        return lax.dot_general(k, q, (((1,), (1,)), ((), ())), preferred_element_type=F32)

    def fold(g, s, j, tiles, q_lo, mask, first):
        if mask is not None:
            s = jnp.where(mask, s, NEG_BIG)
        m_blk = jnp.max(s, axis=0, keepdims=True)
        if first:
            m_new = m_blk
        else:
            m_old = m_ref[g, :, q_lo:]
            m_new = jnp.maximum(m_old, m_blk)
            alpha = jnp.exp2(m_old - m_new)
        p = jnp.exp2(s - m_new).astype(BF16)
        pv = None
        for i in range(tiles):
            vt = jnp.concatenate([vt_ref[j + i, g * V_HEAD_DIM:(g + 1) * V_HEAD_DIM, :], ones], axis=0)
            d = _dot(vt, p[i * tk:(i + 1) * tk, :])
            pv = d if pv is None else pv + d
        p_sum = pv[V_HEAD_DIM:V_HEAD_DIM + 1, :]
        pv = pv[:V_HEAD_DIM, :]
        if first:
            l_ref[g, :, q_lo:] = p_sum
            acc_ref[g, :, q_lo:] = pv
        else:
            l_ref[g, :, q_lo:] = alpha * l_ref[g, :, q_lo:] + p_sum
            acc_ref[g, :, q_lo:] = alpha * acc_ref[g, :, q_lo:] + pv
        m_ref[g, :, q_lo:] = m_new

    def sweep(j, tiles, q_lo, mask, first):
        scores = [qk(g, j, tiles, q_lo) for g in heads]
        for g in heads:
            fold(g, scores[g], j, tiles, q_lo, mask, first)

    for d in range(tq // tk):
        q_lo = d * tk
        key = lax.broadcasted_iota(jnp.int32, (tk, tq - q_lo), 0)
        query = lax.broadcasted_iota(jnp.int32, (tk, tq - q_lo), 1)
        sweep(qi * (tq // tk) + d, 1, q_lo, key // CHUNK <= query // CHUNK, first=(d == 0))

    sweeps = qi * (tq // tk) // ATTN_LOOP_TILES
    odd = sweeps % 2

    @pl.when(odd == 1)
    def _():
        sweep(0, ATTN_LOOP_TILES, 0, None, first=False)

    @pl.loop(0, sweeps // 2)
    def _(trip):
        j = (odd + 2 * trip) * ATTN_LOOP_TILES
        sweep(j, ATTN_LOOP_TILES, 0, None, first=False)
        sweep(j + ATTN_LOOP_TILES, ATTN_LOOP_TILES, 0, None, first=False)

    for g in heads:
        o_ref[:, g * V_HEAD_DIM:(g + 1) * V_HEAD_DIM] = (acc_ref[g] / l_ref[g]).T.astype(o_ref.dtype)


def _attn_call(q, k, vt, batch, seq):
    tq = min(seq, ATTN_TQ)
    nq = seq // tq
    g = ATTN_HEADS_PER_STEP
    assert (tq // ATTN_TK) % ATTN_LOOP_TILES == 0 and seq % tq == 0 and MLA_HEADS % g == 0
    return pl.pallas_call(
        _attn_kernel,
        grid=(batch, MLA_HEADS // g, nq),
        in_specs=[pl.BlockSpec((tq, g * QK_PAD), lambda b, h, i: (b * nq + i, h)),
                  pl.BlockSpec((seq, g * QK_PAD), lambda b, h, i: (b, h)),
                  pl.BlockSpec((seq // ATTN_TK, g * V_HEAD_DIM, ATTN_TK), lambda b, h, i: (b, h, 0))],
        out_specs=pl.BlockSpec((tq, g * V_HEAD_DIM), lambda b, h, i: (b * nq + i, h)),
        out_shape=jax.ShapeDtypeStruct((batch * seq, D_MLA), BF16),
        scratch_shapes=[pltpu.VMEM((g, 1, tq), F32), pltpu.VMEM((g, 1, tq), F32),
                        pltpu.VMEM((g, V_HEAD_DIM, tq), F32)],
        compiler_params=pltpu.CompilerParams(
            dimension_semantics=("parallel", "parallel", "arbitrary"),
            vmem_limit_bytes=V7X_VMEM_LIMIT_BYTES),
        name="attn",
    )(q, k, vt)


def _post_kernel(h_ref, ys_ref, ym_ref, p_ref, wmix_ref, n2_ref, wi_ref, wo_ref,
                 np_ref, wgate_ref, wproj_ref, nf_ref, o_ref, *, final):
    h2 = (h_ref[...] + _dot(ys_ref[...], wmix_ref[:D_SSM, :]) + _dot(ym_ref[...], wmix_ref[D_SSM:, :]))
    xn = _rms(h2, n2_ref[...]).astype(BF16)
    h3 = h2 + 0.5 * _ffn(xn, wi_ref, wo_ref)
    gate = jax.nn.sigmoid(_dot(_rms(h3, np_ref[...]).astype(BF16), wgate_ref[...]))
    h4 = h3 + gate * _dot(p_ref[...].astype(BF16), wproj_ref[...])
    if final:
        h4 = _rms(h4, nf_ref[...])
    o_ref[...] = h4


def _post_call(h, ys, ym, p, params, layer, final):
    t = h.shape[0]
    tm = min(t, TOKEN_TILE)
    in_specs = ([_row_spec(tm, D_MODEL), _row_spec(tm, D_SSM), _row_spec(tm, D_MLA),
                 pl.BlockSpec((None, tm, PLE_DIM), lambda i: (layer, i, 0))]
                + _param_specs(params, 1, layer))
    return pl.pallas_call(
        functools.partial(_post_kernel, final=final),
        grid=(t // tm,),
        in_specs=in_specs,
        out_specs=_row_spec(tm, D_MODEL),
        out_shape=jax.ShapeDtypeStruct((t, D_MODEL), F32),
        compiler_params=pltpu.CompilerParams(
            dimension_semantics=("parallel",), vmem_limit_bytes=V7X_VMEM_LIMIT_BYTES),
        name="post",
    )(h, ys, ym, p, *params)


def _pad_last(w, width):
    return jnp.pad(w, [(0, 0)] * (w.ndim - 1) + [(0, width - w.shape[-1])])


def _layout_in_proj(w):
    o = 0
    w_z = w[..., o:o + D_SSM]; o += D_SSM
    w_xbc = w[..., o:o + CONV_DIM]; o += CONV_DIM
    w_dt = w[..., o:o + SSD_HEADS]; o += SSD_HEADS
    w_qa = w[..., o:o + Q_LORA_RANK]; o += Q_LORA_RANK
    w_kva = w[..., o:o + KV_LORA_RANK]; o += KV_LORA_RANK
    w_kr = w[..., o:o + QK_ROPE_DIM]
    return jnp.concatenate([
        w_z, w_xbc, w_qa, w_kva, _pad_last(w_dt, DT_PAD - QK_ROPE_DIM), w_kr], axis=-1).astype(BF16)


def _layout_q_b(w):
    depth = w.shape[0]
    w = w.reshape(depth, Q_LORA_RANK, MLA_HEADS, QK_NOPE_DIM + QK_ROPE_DIM)
    return (w[..., :QK_NOPE_DIM].reshape(depth, Q_LORA_RANK, MLA_HEADS * QK_NOPE_DIM).astype(BF16),
            w[..., QK_NOPE_DIM:].reshape(depth, Q_LORA_RANK, MLA_HEADS * QK_ROPE_DIM).astype(BF16))


def _layout_kv_b(w):
    depth = w.shape[0]
    w = w.reshape(depth, KV_LORA_RANK, MLA_HEADS, QK_NOPE_DIM + V_HEAD_DIM)
    wk = w[..., :QK_NOPE_DIM].reshape(depth, KV_LORA_RANK, MLA_HEADS * QK_NOPE_DIM)
    wv_t = jnp.swapaxes(w[..., QK_NOPE_DIM:].reshape(depth, KV_LORA_RANK, D_MLA), 1, 2)
    return wk.astype(BF16), wv_t.astype(BF16)


def _rows(v):
    return v.reshape(v.shape[0], 1, -1).astype(F32)


def kernel(x, p, positions, ffn1_norm, ffn1_w_in, ffn1_w_out, mix_norm, w_in_mix, conv_w, conv_b,
           dt_bias, a_log, d_skip, ssd_norm, q_a_norm, w_q_b, kv_a_norm, w_kv_b, w_out_mix,
           ffn2_norm, ffn2_w_in, ffn2_w_out, ple_norm, w_ple_gate, w_ple_proj, final_norm):
    batch, seq, _ = x.shape
    depth = p.shape[0]
    t = batch * seq
    assert seq % SSD_BLOCK == 0 and seq % min(seq, ATTN_TQ) == 0 and t % TOKEN_TILE == 0

    rope = _rope_tables(positions)
    q_idx = jnp.arange(SSD_BLOCK)
    tri = (q_idx[None, :] <= q_idx[:, None]).astype(BF16)
    expand = (jnp.arange(D_SSM)[None, :] // SSD_HEAD_DIM == jnp.arange(DT_PAD)[:, None]).astype(BF16)
    expand = jnp.concatenate([expand, expand], axis=0)

    wqn, wqr = _layout_q_b(w_q_b)
    wk, wv = _layout_kv_b(w_kv_b)
    pre_params = [_rows(ffn1_norm), ffn1_w_in.astype(BF16), ffn1_w_out.astype(BF16), _rows(mix_norm),
                  _layout_in_proj(w_in_mix), _rows(q_a_norm), _rows(kv_a_norm), wqn, wqr, wk, wv,
                  conv_w.astype(F32), _rows(conv_b)]
    ssd_params = [_pad_last(_rows(dt_bias), DT_PAD), _pad_last(_rows(a_log), DT_PAD),
                  _rows(jnp.repeat(d_skip, SSD_HEAD_DIM, axis=-1)), _rows(ssd_norm), tri, expand]
    post_params = [w_out_mix.astype(BF16), _rows(ffn2_norm), ffn2_w_in.astype(BF16),
                   ffn2_w_out.astype(BF16), _rows(ple_norm), w_ple_gate.astype(BF16),
                   w_ple_proj.astype(BF16), final_norm.reshape(1, -1).astype(F32)]
    p = p.reshape(depth, t, PLE_DIM)

    h = x.reshape(t, D_MODEL)
    for i in range(depth):
        h, zg, xbc, dtm, q, k, v = _pre_call(h, pre_params, rope, seq, i)
        y_ssd = _ssd_call(xbc, zg, dtm, ssd_params, batch, seq, i)
        y_mla = _attn_call(q, k, v, batch, seq)
        h = _post_call(h, y_ssd, y_mla, p, post_params, i, final=(i == depth - 1))
    return h.reshape(batch, seq, D_MODEL)
```

```python
import functools

import jax
import jax.numpy as jnp
from jax import lax
from jax.experimental import pallas as pl
from jax.experimental.pallas import tpu as pltpu

F32 = jnp.float32
BF16 = jnp.bfloat16

D_MODEL = 1024
CHUNK = 64
PLE_DIM = 256
D_FF = 2816
EPS = 1e-6
SSD_HEADS = 16
SSD_HEAD_DIM = 64
D_SSM = SSD_HEADS * SSD_HEAD_DIM
SSD_GROUPS = 2
SSD_HEADS_PER_GROUP = SSD_HEADS // SSD_GROUPS
SSD_STATE = 128
CONV_WIDTH = 4
CONV_DIM = D_SSM + 2 * SSD_GROUPS * SSD_STATE
MLA_HEADS = 8
QK_NOPE_DIM = 128
QK_ROPE_DIM = 64
V_HEAD_DIM = 128
Q_LORA_RANK = 384
KV_LORA_RANK = 256
D_MLA = MLA_HEADS * V_HEAD_DIM
ROPE_THETA = 10000.0
SOFTMAX_SCALE = (QK_NOPE_DIM + QK_ROPE_DIM) ** -0.5
LOG2_E = 1.4426950408889634

V7X_LANES = 128
V7X_SUBLANES = 8
V7X_MXU_DIM = 256
V7X_VMEM_LIMIT_BYTES = 56 * 1024 * 1024

QK_PAD = V7X_MXU_DIM
ROPE_HALF = QK_ROPE_DIM // 2
ROPE_TILE = V7X_LANES
DT_PAD = V7X_LANES
FF_CHUNK = V7X_MXU_DIM
SSD_BLOCK = 256
CONV_HIST = V7X_SUBLANES
TOKEN_TILE = 512
ATTN_TQ = 512
ATTN_TK = 256
ATTN_HEADS_PER_STEP = 4
ATTN_LOOP_TILES = 2
ATTN_ONES_ROWS = 2 * V7X_SUBLANES
NEG_BIG = -1e30

_C_Z = 0
_C_XBC = _C_Z + D_SSM
_C_QA = _C_XBC + CONV_DIM
_C_KVA = _C_QA + Q_LORA_RANK
_C_DT = _C_KVA + KV_LORA_RANK
_C_END = _C_DT + DT_PAD


def _dot(a, b):
    return jnp.dot(a, b, preferred_element_type=F32)


def _rms(x, w):
    return x * lax.rsqrt(jnp.mean(x * x, axis=-1, keepdims=True) + EPS) * w


def _ffn(xn, wi_ref, wo_ref):
    n = D_FF // FF_CHUNK
    cols = lambda c: slice(c * FF_CHUNK, (c + 1) * FF_CHUNK)
    up_cols = lambda c: slice(D_FF + c * FF_CHUNK, D_FF + (c + 1) * FF_CHUNK)
    up = lambda c: (_dot(xn, wi_ref[:, cols(c)]), _dot(xn, wi_ref[:, up_cols(c)]))
    acc = None
    gu = up(0)
    for c in range(n):
        gu_next = up(c + 1) if c + 1 < n else None
        g, u = gu
        a = (g * jax.nn.sigmoid(g) * u).astype(BF16)
        d = _dot(a, wo_ref[cols(c), :])
        acc = d if acc is None else acc + d
        gu = gu_next
    return acc


def _const_spec(shape, grid_rank, layer=None):
    index = (0,) * len(shape) if layer is None else (layer,) + (0,) * (len(shape) - 1)
    block = tuple(shape) if layer is None else (None,) + tuple(shape[1:])
    if grid_rank == 1:
        index_map = lambda i: index
    elif grid_rank == 2:
        index_map = lambda i, j: index
    else:
        index_map = lambda i, j, k: index
    return pl.BlockSpec(block, index_map, pipeline_mode=pl.Buffered(1))


def _param_specs(params, grid_rank, layer):
    return [_const_spec(a.shape, grid_rank, layer if a.ndim == 3 else None) for a in params]


def _row_spec(tm, width):
    return pl.BlockSpec((tm, width), lambda i: (i, 0))


def _rope_kernel(pos_ref, inv_ref, lo_ref, hi_ref, c_ref, slo_ref, shi_ref):
    ang = pos_ref[...].astype(F32) * inv_ref[...]
    sin = jnp.sin(ang)
    c_ref[...] = jnp.cos(ang)
    slo_ref[...] = sin * lo_ref[...]
    shi_ref[...] = sin * hi_ref[...]


def _rope_tables(positions):
    t = positions.size
    tm = min(t, 1024)
    groups = ROPE_TILE // QK_ROPE_DIM
    inv = ROPE_THETA ** (-jnp.arange(0, QK_ROPE_DIM, 2, dtype=F32) / QK_ROPE_DIM)
    ones_h = jnp.ones((ROPE_HALF,), F32)
    zeros_h = jnp.zeros((ROPE_HALF,), F32)
    inv_row = jnp.tile(inv, 2 * groups)[None, :]
    lo = jnp.tile(jnp.concatenate([-ones_h, zeros_h]), groups)[None, :]
    hi = jnp.tile(jnp.concatenate([zeros_h, ones_h]), groups)[None, :]
    row = _const_spec((1, ROPE_TILE), 1)
    return pl.pallas_call(
        _rope_kernel,
        grid=(t // tm,),
        in_specs=[pl.BlockSpec((tm, 1), lambda i: (i, 0)), row, row, row],
        out_specs=[_row_spec(tm, ROPE_TILE)] * 3,
        out_shape=[jax.ShapeDtypeStruct((t, ROPE_TILE), F32)] * 3,
        compiler_params=pltpu.CompilerParams(dimension_semantics=("parallel",)),
        name="rope_tables",
    )(positions.reshape(t, 1), inv_row, lo, hi)


def _rotate(x, c, s_lo, s_hi):
    ahead = pltpu.roll(x, shift=ROPE_TILE - ROPE_HALF, axis=1)
    behind = pltpu.roll(x, shift=ROPE_HALF, axis=1)
    return x * c + ahead * s_lo + behind * s_hi


def _silu(x):
    return x * jax.nn.sigmoid(x)


def _pre_kernel(x_ref, n1_ref, wi_ref, wo_ref, nm_ref, win_ref, qn_ref, kn_ref,
                wqn_ref, wqr_ref, wk_ref, wv_ref, cw_ref, cb_ref, c_ref, slo_ref, shi_ref,
                h_ref, zg_ref, xbc_ref, dt_ref, q_ref, k_ref, v_ref, raw_ref, hist_ref,
                *, tiles_per_seq):
    i = pl.program_id(0)

    @pl.when(i == 0)
    def _():
        raw_ref[...] = jnp.zeros_like(raw_ref)
        hist_ref[...] = jnp.zeros_like(hist_ref)

    prev = raw_ref[...]
    seq_start = lax.rem(i - 1, tiles_per_seq) == 0
    hist = jnp.where(seq_start, 0.0, hist_ref[...])
    xpad = jnp.concatenate([hist, prev], axis=0)
    conv = cb_ref[...] + cw_ref[CONV_WIDTH - 1:CONV_WIDTH, :] * prev
    for back in range(1, CONV_WIDTH):
        shifted = pltpu.roll(xpad, shift=back, axis=0)[CONV_HIST:, :]
        conv = conv + cw_ref[CONV_WIDTH - 1 - back:CONV_WIDTH - back, :] * shifted
    xbc_ref[...] = _silu(conv)
    hist_ref[...] = prev[prev.shape[0] - CONV_HIST:, :]

    x = x_ref[...]
    xn = _rms(x, n1_ref[...]).astype(BF16)
    h1 = x + 0.5 * _ffn(xn, wi_ref, wo_ref)
    h_ref[...] = h1
    u = _rms(h1, nm_ref[...]).astype(BF16)
    tail = _dot(u, win_ref[:, _C_QA:_C_END])
    misc = tail[:, _C_DT - _C_QA:]
    dt_ref[...] = misc
    qn = _rms(tail[:, :_C_KVA - _C_QA], qn_ref[...]).astype(BF16)
    kvn = _rms(tail[:, _C_KVA - _C_QA:_C_DT - _C_QA], kn_ref[...]).astype(BF16)
    c, s_lo, s_hi = c_ref[...], slo_ref[...], shi_ref[...]
    lane = lax.broadcasted_iota(jnp.int32, (1, ROPE_TILE), 1)
    first_group = lane < QK_ROPE_DIM
    kr = jnp.where(first_group, pltpu.roll(_rotate(misc, c, s_lo, s_hi), shift=QK_ROPE_DIM, axis=1),
                   0.0).astype(BF16)
    q_scale = SOFTMAX_SCALE * LOG2_E
    q_nope = _dot(qn, wqn_ref[...]) * q_scale
    k_nope = _dot(kvn, wk_ref[...])
    q_rope = _dot(qn, wqr_ref[...])
    for pair in range(MLA_HEADS // 2):
        tile = slice(pair * ROPE_TILE, (pair + 1) * ROPE_TILE)
        rot = _rotate(q_rope[:, tile], c, s_lo, s_hi) * q_scale
        for sub, part in enumerate((jnp.where(first_group, rot, 0.0),
                                    jnp.where(first_group, pltpu.roll(rot, shift=QK_ROPE_DIM, axis=1), 0.0))):
            base = (2 * pair + sub) * QK_PAD
            q_ref[:, base + QK_NOPE_DIM:base + QK_PAD] = part.astype(BF16)
    for h in range(MLA_HEADS):
        nope = slice(h * QK_NOPE_DIM, (h + 1) * QK_NOPE_DIM)
        q_ref[:, h * QK_PAD:h * QK_PAD + QK_NOPE_DIM] = q_nope[:, nope].astype(BF16)
        k_ref[:, h * QK_PAD:h * QK_PAD + QK_NOPE_DIM] = k_nope[:, nope].astype(BF16)
        k_ref[:, h * QK_PAD + QK_NOPE_DIM:(h + 1) * QK_PAD] = kr
    v_ref[0] = lax.dot_general(wv_ref[...], kvn, (((1,), (1,)), ((), ())),
                               preferred_element_type=F32).astype(BF16)
    zg_ref[...] = _silu(_dot(u, win_ref[:, _C_Z:_C_XBC]))
    raw_ref[...] = _dot(u, win_ref[:, _C_XBC:_C_QA])


def _pre_call(h, params, rope, seq, layer):
    t = h.shape[0]
    tm = ATTN_TK
    n = t // tm
    cur = lambda i: (jnp.minimum(i, n - 1), 0)
    prev = lambda i: (jnp.maximum(i - 1, 0), 0)
    in_specs = ([pl.BlockSpec((tm, D_MODEL), cur)] + _param_specs(params, 1, layer)
                + [pl.BlockSpec((tm, ROPE_TILE), cur)] * len(rope))
    outs = [(D_MODEL, F32, cur), (D_SSM, F32, cur), (CONV_DIM, F32, prev), (DT_PAD, F32, cur),
            (MLA_HEADS * QK_PAD, BF16, cur), (MLA_HEADS * QK_PAD, BF16, cur)]
    return pl.pallas_call(
        functools.partial(_pre_kernel, tiles_per_seq=seq // tm),
        grid=(n + 1,),
        in_specs=in_specs,
        out_specs=[pl.BlockSpec((tm, w), im) for w, _, im in outs]
        + [pl.BlockSpec((1, D_MLA, tm), lambda i: (jnp.minimum(i, n - 1), 0, 0))],
        out_shape=[jax.ShapeDtypeStruct((t, w), d) for w, d, _ in outs]
        + [jax.ShapeDtypeStruct((n, D_MLA, tm), BF16)],
        scratch_shapes=[pltpu.VMEM((tm, CONV_DIM), F32), pltpu.VMEM((CONV_HIST, CONV_DIM), F32)],
        compiler_params=pltpu.CompilerParams(
            dimension_semantics=("arbitrary",), vmem_limit_bytes=V7X_VMEM_LIMIT_BYTES),
        name="pre",
    )(h, *params, *rope)


def _split_dot(v, m, parts, left=False):
    out = None
    rem = v
    for _ in range(parts):
        piece = rem.astype(BF16)
        rem = rem - piece.astype(F32)
        d = _dot(m, piece) if left else _dot(piece, m)
        out = d if out is None else out + d
    return out


def _softplus(x):
    return jnp.maximum(x, 0.0) + jnp.log1p(jnp.exp(-jnp.abs(x)))


def _ssd_block(xbc_ref, zg_ref, dt_ref, dtb_ref, alog_ref, dskip_ref, nw_ref,
               tri_ref, exp_ref, y_ref, state_ref):
    q = SSD_BLOCK
    hp = SSD_HEADS_PER_GROUP * SSD_HEAD_DIM
    xbc = xbc_ref[...]
    xs = xbc[:, :D_SSM]

    lane = lax.broadcasted_iota(jnp.int32, (1, DT_PAD), 1)
    head_lane = lane < SSD_HEADS
    dt = jnp.where(head_lane, _softplus(dt_ref[...] + dtb_ref[...]), 0.0)
    a_neg = jnp.where(head_lane, -jnp.exp(alog_ref[...]), 0.0)
    a_cum = _split_dot(dt * (a_neg * LOG2_E), tri_ref[...], 3, left=True)
    a_cum_t = a_cum.T
    a_last = a_cum[q - 1:q, :]
    to_end = jnp.exp2(a_last - a_cum)
    from_start = jnp.exp2(a_cum)

    def expand(v):
        hi = v.astype(BF16)
        lo = (v - hi.astype(F32)).astype(BF16)
        return _dot(jnp.concatenate([hi, lo], axis=1), exp_ref[...])

    dt_e = expand(dt)
    w_e = expand(dt * to_end)
    fs_e = expand(from_start)
    xdt = (xs * dt_e).astype(BF16)
    xw = (xs * w_e).astype(BF16)

    row = lax.broadcasted_iota(jnp.int32, (q, q), 0)
    col = lax.broadcasted_iota(jnp.int32, (q, q), 1)
    causal = col <= row
    lane_pair = lax.broadcasted_iota(jnp.int32, (1, 2 * SSD_HEAD_DIM), 1)
    first_of_pair = lane_pair < SSD_HEAD_DIM

    zg = zg_ref[...]
    for g in range(SSD_GROUPS):
        yield
        b_g = xbc[:, D_SSM + g * SSD_STATE:D_SSM + (g + 1) * SSD_STATE]
        c_g = xbc[:, D_SSM + (SSD_GROUPS + g) * SSD_STATE:D_SSM + (SSD_GROUPS + g + 1) * SSD_STATE]
        b_bf = b_g.astype(BF16)
        c_bf = c_g.astype(BF16)
        cb = lax.dot_general(c_bf, b_bf, (((1,), (1,)), ((), ())), preferred_element_type=F32)
        gsl = slice(g * hp, (g + 1) * hp)
        state = state_ref[g]
        y_g = _dot(c_bf, state.astype(BF16)) * fs_e[:, gsl] + xs[:, gsl] * dskip_ref[:, gsl]
        pieces = []
        for pair in range(SSD_HEADS_PER_GROUP // 2):
            outs = []
            for sub in range(2):
                h = g * SSD_HEADS_PER_GROUP + 2 * pair + sub
                seg = a_cum[:, h:h + 1] - a_cum_t[h:h + 1, :]
                m = (cb * jnp.exp2(jnp.where(causal, seg, NEG_BIG))).astype(BF16)
                psl = slice(g * hp + pair * 2 * SSD_HEAD_DIM, g * hp + (pair + 1) * 2 * SSD_HEAD_DIM)
                outs.append(_dot(m, xdt[:, psl]))
            pieces.append(jnp.where(first_of_pair, outs[0], outs[1]))
        y_g = y_g + jnp.concatenate(pieces, axis=-1)
        y_g = y_g * zg[:, gsl]
        y_g = y_g * lax.rsqrt(jnp.mean(y_g * y_g, axis=-1, keepdims=True) + EPS) * nw_ref[:, gsl]
        y_ref[:, gsl] = y_g.astype(y_ref.dtype)
        state_ref[g] = state * fs_e[q - 1:q, gsl] + _dot(b_g.T.astype(BF16), xw[:, gsl])


def _mix_kernel(q_ref, k_ref, vt_ref, xbc_ref, zg_ref, dt_ref, dtb_ref, alog_ref, dskip_ref, nw_ref,
                tri_ref, exp_ref, o_ref, y_ref, m_ref, l_ref, acc_ref, state_ref):
    @pl.when((pl.program_id(1) == 0) & (pl.program_id(2) == 0))
    def _():
        state_ref[...] = jnp.zeros_like(state_ref)

    ssd = _ssd_block(xbc_ref, zg_ref, dt_ref, dtb_ref, alog_ref, dskip_ref, nw_ref, tri_ref, exp_ref,
                     y_ref, state_ref)
    _attn_body(q_ref, k_ref, vt_ref, o_ref, m_ref, l_ref, acc_ref, ssd)


def _attn_body(q_ref, k_ref, vt_ref, o_ref, m_ref, l_ref, acc_ref, side_work):
    tq, tk = q_ref.shape[0], ATTN_TK
    qi = pl.program_id(2)
    heads = range(ATTN_HEADS_PER_STEP)
    ones = jnp.ones((ATTN_ONES_ROWS, tk), BF16)

    def qk(g, j, tiles, q_lo):
        start = pl.multiple_of(j * tk, tk)
        k = k_ref[pl.ds(start, tiles * tk), g * QK_PAD:(g + 1) * QK_PAD]
        q = q_ref[q_lo:, g * QK_PAD:(g + 1) * QK_PAD]
        return lax.dot_general(k, q, (((1,), (1,)), ((), ())), preferred_element_type=F32)

    def fold(g, s, j, tiles, q_lo, mask, first):
        if mask is not None:
            s = jnp.where(mask, s, NEG_BIG)
        m_blk = jnp.max(s, axis=0, keepdims=True)
        if first:
            m_new = m_blk
        else:
            m_old = m_ref[g, :, q_lo:]
            m_new = jnp.maximum(m_old, m_blk)
            alpha = jnp.exp2(m_old - m_new)
        p = jnp.exp2(s - m_new).astype(BF16)
        pv = None
        for i in range(tiles):
            vt = jnp.concatenate([vt_ref[j + i, g * V_HEAD_DIM:(g + 1) * V_HEAD_DIM, :], ones], axis=0)
            d = _dot(vt, p[i * tk:(i + 1) * tk, :])
            pv = d if pv is None else pv + d
        p_sum = pv[V_HEAD_DIM:V_HEAD_DIM + 1, :]
        pv = pv[:V_HEAD_DIM, :]
        if first:
            l_ref[g, :, q_lo:] = p_sum
            acc_ref[g, :, q_lo:] = pv
        else:
            l_ref[g, :, q_lo:] = alpha * l_ref[g, :, q_lo:] + p_sum
            acc_ref[g, :, q_lo:] = alpha * acc_ref[g, :, q_lo:] + pv
        m_ref[g, :, q_lo:] = m_new

    def sweep(j, tiles, q_lo, mask, first):
        scores = [qk(g, j, tiles, q_lo) for g in heads]
        for g in heads:
            fold(g, scores[g], j, tiles, q_lo, mask, first)

    for d in range(tq // tk):
        q_lo = d * tk
        key = lax.broadcasted_iota(jnp.int32, (tk, tq - q_lo), 0)
        query = lax.broadcasted_iota(jnp.int32, (tk, tq - q_lo), 1)
        next(side_work, None)
        sweep(qi * (tq // tk) + d, 1, q_lo, key // CHUNK <= query // CHUNK, first=(d == 0))
    for _ in side_work:
        pass

    sweeps = qi * (tq // tk) // ATTN_LOOP_TILES
    odd = sweeps % 2

    @pl.when(odd == 1)
    def _():
        sweep(0, ATTN_LOOP_TILES, 0, None, first=False)

    @pl.loop(0, sweeps // 2)
    def _(trip):
        j = (odd + 2 * trip) * ATTN_LOOP_TILES
        sweep(j, ATTN_LOOP_TILES, 0, None, first=False)
        sweep(j + ATTN_LOOP_TILES, ATTN_LOOP_TILES, 0, None, first=False)

    for g in heads:
        o_ref[:, g * V_HEAD_DIM:(g + 1) * V_HEAD_DIM] = (acc_ref[g] / l_ref[g]).T.astype(o_ref.dtype)


def _mix_call(q, k, vt, xbc, zg, dtm, params, batch, seq, layer):
    tq = min(seq, ATTN_TQ)
    nq = seq // tq
    g = ATTN_HEADS_PER_STEP
    groups = MLA_HEADS // g
    nblk = seq // SSD_BLOCK
    hp = SSD_HEADS_PER_GROUP * SSD_HEAD_DIM
    assert (tq // ATTN_TK) % ATTN_LOOP_TILES == 0 and seq % tq == 0 and MLA_HEADS % g == 0
    assert groups * nq == nblk
    ssd_blk = lambda w: pl.BlockSpec((SSD_BLOCK, w), lambda b, h, i: (b * nblk + h * nq + i, 0))
    return pl.pallas_call(
        _mix_kernel,
        grid=(batch, groups, nq),
        in_specs=[pl.BlockSpec((tq, g * QK_PAD), lambda b, h, i: (b * nq + i, h)),
                  pl.BlockSpec((seq, g * QK_PAD), lambda b, h, i: (b, h)),
                  pl.BlockSpec((seq // ATTN_TK, g * V_HEAD_DIM, ATTN_TK), lambda b, h, i: (b, h, 0)),
                  ssd_blk(CONV_DIM), ssd_blk(D_SSM), ssd_blk(DT_PAD)] + _param_specs(params, 3, layer),
        out_specs=[pl.BlockSpec((tq, g * V_HEAD_DIM), lambda b, h, i: (b * nq + i, h)), ssd_blk(D_SSM)],
        out_shape=[jax.ShapeDtypeStruct((batch * seq, D_MLA), BF16),
                   jax.ShapeDtypeStruct((batch * seq, D_SSM), BF16)],
        scratch_shapes=[pltpu.VMEM((g, 1, tq), F32), pltpu.VMEM((g, 1, tq), F32),
                        pltpu.VMEM((g, V_HEAD_DIM, tq), F32),
                        pltpu.VMEM((SSD_GROUPS, SSD_STATE, hp), F32)],
        compiler_params=pltpu.CompilerParams(
            dimension_semantics=("arbitrary", "arbitrary", "arbitrary"),
            vmem_limit_bytes=V7X_VMEM_LIMIT_BYTES),
        name="mix",
    )(q, k, vt, xbc, zg, dtm, *params)


def _post_kernel(h_ref, ys_ref, ym_ref, p_ref, wmix_ref, n2_ref, wi_ref, wo_ref,
                 np_ref, wgate_ref, wproj_ref, nf_ref, o_ref, *, final):
    h2 = (h_ref[...] + _dot(ys_ref[...], wmix_ref[:D_SSM, :]) + _dot(ym_ref[...], wmix_ref[D_SSM:, :]))
    xn = _rms(h2, n2_ref[...]).astype(BF16)
    h3 = h2 + 0.5 * _ffn(xn, wi_ref, wo_ref)
    gate = jax.nn.sigmoid(_dot(_rms(h3, np_ref[...]).astype(BF16), wgate_ref[...]))
    h4 = h3 + gate * _dot(p_ref[...].astype(BF16), wproj_ref[...])
    if final:
        h4 = _rms(h4, nf_ref[...])
    o_ref[...] = h4


def _post_call(h, ys, ym, p, params, layer, final):
    t = h.shape[0]
    tm = min(t, TOKEN_TILE)
    in_specs = ([_row_spec(tm, D_MODEL), _row_spec(tm, D_SSM), _row_spec(tm, D_MLA),
                 pl.BlockSpec((None, tm, PLE_DIM), lambda i: (layer, i, 0))]
                + _param_specs(params, 1, layer))
    return pl.pallas_call(
        functools.partial(_post_kernel, final=final),
        grid=(t // tm,),
        in_specs=in_specs,
        out_specs=_row_spec(tm, D_MODEL),
        out_shape=jax.ShapeDtypeStruct((t, D_MODEL), F32),
        compiler_params=pltpu.CompilerParams(
            dimension_semantics=("parallel",), vmem_limit_bytes=V7X_VMEM_LIMIT_BYTES),
        name="post",
    )(h, ys, ym, p, *params)


def _pad_last(w, width):
    return jnp.pad(w, [(0, 0)] * (w.ndim - 1) + [(0, width - w.shape[-1])])


def _layout_in_proj(w):
    o = 0
    w_z = w[..., o:o + D_SSM]; o += D_SSM
    w_xbc = w[..., o:o + CONV_DIM]; o += CONV_DIM
    w_dt = w[..., o:o + SSD_HEADS]; o += SSD_HEADS
    w_qa = w[..., o:o + Q_LORA_RANK]; o += Q_LORA_RANK
    w_kva = w[..., o:o + KV_LORA_RANK]; o += KV_LORA_RANK
    w_kr = w[..., o:o + QK_ROPE_DIM]
    return jnp.concatenate([
        w_z, w_xbc, w_qa, w_kva, _pad_last(w_dt, DT_PAD - QK_ROPE_DIM), w_kr], axis=-1).astype(BF16)


def _layout_q_b(w):
    depth = w.shape[0]
    w = w.reshape(depth, Q_LORA_RANK, MLA_HEADS, QK_NOPE_DIM + QK_ROPE_DIM)
    return (w[..., :QK_NOPE_DIM].reshape(depth, Q_LORA_RANK, MLA_HEADS * QK_NOPE_DIM).astype(BF16),
            w[..., QK_NOPE_DIM:].reshape(depth, Q_LORA_RANK, MLA_HEADS * QK_ROPE_DIM).astype(BF16))


def _layout_kv_b(w):
    depth = w.shape[0]
    w = w.reshape(depth, KV_LORA_RANK, MLA_HEADS, QK_NOPE_DIM + V_HEAD_DIM)
    wk = w[..., :QK_NOPE_DIM].reshape(depth, KV_LORA_RANK, MLA_HEADS * QK_NOPE_DIM)
    wv_t = jnp.swapaxes(w[..., QK_NOPE_DIM:].reshape(depth, KV_LORA_RANK, D_MLA), 1, 2)
    return wk.astype(BF16), wv_t.astype(BF16)


def _rows(v):
    return v.reshape(v.shape[0], 1, -1).astype(F32)


def kernel(x, p, positions, ffn1_norm, ffn1_w_in, ffn1_w_out, mix_norm, w_in_mix, conv_w, conv_b,
           dt_bias, a_log, d_skip, ssd_norm, q_a_norm, w_q_b, kv_a_norm, w_kv_b, w_out_mix,
           ffn2_norm, ffn2_w_in, ffn2_w_out, ple_norm, w_ple_gate, w_ple_proj, final_norm):
    batch, seq, _ = x.shape
    depth = p.shape[0]
    t = batch * seq
    assert seq % SSD_BLOCK == 0 and seq % min(seq, ATTN_TQ) == 0 and t % TOKEN_TILE == 0

    rope = _rope_tables(positions)
    q_idx = jnp.arange(SSD_BLOCK)
    tri = (q_idx[None, :] <= q_idx[:, None]).astype(BF16)
    expand = (jnp.arange(D_SSM)[None, :] // SSD_HEAD_DIM == jnp.arange(DT_PAD)[:, None]).astype(BF16)
    expand = jnp.concatenate([expand, expand], axis=0)

    wqn, wqr = _layout_q_b(w_q_b)
    wk, wv = _layout_kv_b(w_kv_b)
    pre_params = [_rows(ffn1_norm), ffn1_w_in.astype(BF16), ffn1_w_out.astype(BF16), _rows(mix_norm),
                  _layout_in_proj(w_in_mix), _rows(q_a_norm), _rows(kv_a_norm), wqn, wqr, wk, wv,
                  conv_w.astype(F32), _rows(conv_b)]
    ssd_params = [_pad_last(_rows(dt_bias), DT_PAD), _pad_last(_rows(a_log), DT_PAD),
                  _rows(jnp.repeat(d_skip, SSD_HEAD_DIM, axis=-1)), _rows(ssd_norm), tri, expand]
    post_params = [w_out_mix.astype(BF16), _rows(ffn2_norm), ffn2_w_in.astype(BF16),
                   ffn2_w_out.astype(BF16), _rows(ple_norm), w_ple_gate.astype(BF16),
                   w_ple_proj.astype(BF16), final_norm.reshape(1, -1).astype(F32)]
    p = p.reshape(depth, t, PLE_DIM)

    h = x.reshape(t, D_MODEL)
    for i in range(depth):
        h, zg, xbc, dtm, q, k, v = _pre_call(h, pre_params, rope, seq, i)
        y_mla, y_ssd = _mix_call(q, k, v, xbc, zg, dtm, ssd_params, batch, seq, i)
        h = _post_call(h, y_ssd, y_mla, p, post_params, i, final=(i == depth - 1))
    return h.reshape(batch, seq, D_MODEL)
```

```python
import functools

import jax
import jax.numpy as jnp
from jax import lax
from jax.experimental import pallas as pl
from jax.experimental.pallas import tpu as pltpu

F32 = jnp.float32
BF16 = jnp.bfloat16

D_MODEL = 1024
CHUNK = 64
PLE_DIM = 256
D_FF = 2816
EPS = 1e-6
SSD_HEADS = 16
SSD_HEAD_DIM = 64
D_SSM = SSD_HEADS * SSD_HEAD_DIM
SSD_GROUPS = 2
SSD_HEADS_PER_GROUP = SSD_HEADS // SSD_GROUPS
SSD_STATE = 128
CONV_WIDTH = 4
CONV_DIM = D_SSM + 2 * SSD_GROUPS * SSD_STATE
MLA_HEADS = 8
QK_NOPE_DIM = 128
QK_ROPE_DIM = 64
V_HEAD_DIM = 128
Q_LORA_RANK = 384
KV_LORA_RANK = 256
D_MLA = MLA_HEADS * V_HEAD_DIM
ROPE_THETA = 10000.0
SOFTMAX_SCALE = (QK_NOPE_DIM + QK_ROPE_DIM) ** -0.5
LOG2_E = 1.4426950408889634

V7X_LANES = 128
V7X_SUBLANES = 8
V7X_MXU_DIM = 256
V7X_VMEM_LIMIT_BYTES = 56 * 1024 * 1024

QK_PAD = V7X_MXU_DIM
ROPE_HALF = QK_ROPE_DIM // 2
ROPE_TILE = V7X_LANES
DT_PAD = V7X_LANES
FF_CHUNK = V7X_MXU_DIM
SSD_BLOCK = 256
CONV_HIST = V7X_SUBLANES
TOKEN_TILE = 512
ATTN_TQ = 512
ATTN_TK = 256
ATTN_HEADS_PER_STEP = 4
ATTN_LOOP_TILES = 2
ATTN_ONES_ROWS = 2 * V7X_SUBLANES
NEG_BIG = -1e30

_C_Z = 0
_C_XBC = _C_Z + D_SSM
_C_QA = _C_XBC + CONV_DIM
_C_KVA = _C_QA + Q_LORA_RANK
_C_DT = _C_KVA + KV_LORA_RANK
_C_END = _C_DT + DT_PAD


def _dot(a, b):
    return jnp.dot(a, b, preferred_element_type=F32)


def _rms(x, w):
    return x * lax.rsqrt(jnp.mean(x * x, axis=-1, keepdims=True) + EPS) * w


def _ffn(xn, wi_ref, wo_ref):
    n = D_FF // FF_CHUNK
    cols = lambda c: slice(c * FF_CHUNK, (c + 1) * FF_CHUNK)
    up_cols = lambda c: slice(D_FF + c * FF_CHUNK, D_FF + (c + 1) * FF_CHUNK)
    up = lambda c: (_dot(xn, wi_ref[:, cols(c)]), _dot(xn, wi_ref[:, up_cols(c)]))
    acc = None
    gu = up(0)
    for c in range(n):
        gu_next = up(c + 1) if c + 1 < n else None
        g, u = gu
        a = (g * jax.nn.sigmoid(g) * u).astype(BF16)
        d = _dot(a, wo_ref[cols(c), :])
        acc = d if acc is None else acc + d
        gu = gu_next
    return acc


def _const_spec(shape, grid_rank, layer=None):
    index = (0,) * len(shape) if layer is None else (layer,) + (0,) * (len(shape) - 1)
    block = tuple(shape) if layer is None else (None,) + tuple(shape[1:])
    if grid_rank == 1:
        index_map = lambda i: index
    elif grid_rank == 2:
        index_map = lambda i, j: index
    else:
        index_map = lambda i, j, k: index
    return pl.BlockSpec(block, index_map, pipeline_mode=pl.Buffered(1))


def _param_specs(params, grid_rank, layer):
    return [_const_spec(a.shape, grid_rank, layer if a.ndim == 3 else None) for a in params]


def _row_spec(tm, width):
    return pl.BlockSpec((tm, width), lambda i: (i, 0))


def _rope_kernel(pos_ref, inv_ref, lo_ref, hi_ref, c_ref, slo_ref, shi_ref):
    ang = pos_ref[...].astype(F32) * inv_ref[...]
    sin = jnp.sin(ang)
    c_ref[...] = jnp.cos(ang)
    slo_ref[...] = sin * lo_ref[...]
    shi_ref[...] = sin * hi_ref[...]


def _rope_tables(positions):
    t = positions.size
    tm = min(t, 1024)
    groups = ROPE_TILE // QK_ROPE_DIM
    inv = ROPE_THETA ** (-jnp.arange(0, QK_ROPE_DIM, 2, dtype=F32) / QK_ROPE_DIM)
    ones_h = jnp.ones((ROPE_HALF,), F32)
    zeros_h = jnp.zeros((ROPE_HALF,), F32)
    inv_row = jnp.tile(inv, 2 * groups)[None, :]
    lo = jnp.tile(jnp.concatenate([-ones_h, zeros_h]), groups)[None, :]
    hi = jnp.tile(jnp.concatenate([zeros_h, ones_h]), groups)[None, :]
    row = _const_spec((1, ROPE_TILE), 1)
    return pl.pallas_call(
        _rope_kernel,
        grid=(t // tm,),
        in_specs=[pl.BlockSpec((tm, 1), lambda i: (i, 0)), row, row, row],
        out_specs=[_row_spec(tm, ROPE_TILE)] * 3,
        out_shape=[jax.ShapeDtypeStruct((t, ROPE_TILE), F32)] * 3,
        compiler_params=pltpu.CompilerParams(dimension_semantics=("parallel",)),
        name="rope_tables",
    )(positions.reshape(t, 1), inv_row, lo, hi)


def _rotate(x, c, s_lo, s_hi):
    ahead = pltpu.roll(x, shift=ROPE_TILE - ROPE_HALF, axis=1)
    behind = pltpu.roll(x, shift=ROPE_HALF, axis=1)
    return x * c + ahead * s_lo + behind * s_hi


def _silu(x):
    return x * jax.nn.sigmoid(x)


def _pre_kernel(x_ref, n1_ref, wi_ref, wo_ref, nm_ref, win_ref, qn_ref, kn_ref,
                wqn_ref, wqr_ref, wk_ref, wv_ref, cw_ref, cb_ref, c_ref, slo_ref, shi_ref,
                h_ref, zg_ref, xbc_ref, dt_ref, q_ref, k_ref, v_ref, raw_ref, hist_ref,
                *, tiles_per_seq):
    i = pl.program_id(0)

    @pl.when(i == 0)
    def _():
        raw_ref[...] = jnp.zeros_like(raw_ref)
        hist_ref[...] = jnp.zeros_like(hist_ref)

    prev = raw_ref[...]
    seq_start = lax.rem(i - 1, tiles_per_seq) == 0
    hist = jnp.where(seq_start, 0.0, hist_ref[...])
    xpad = jnp.concatenate([hist, prev], axis=0)
    conv = cb_ref[...] + cw_ref[CONV_WIDTH - 1:CONV_WIDTH, :] * prev
    for back in range(1, CONV_WIDTH):
        shifted = pltpu.roll(xpad, shift=back, axis=0)[CONV_HIST:, :]
        conv = conv + cw_ref[CONV_WIDTH - 1 - back:CONV_WIDTH - back, :] * shifted
    xbc_ref[...] = _silu(conv)
    hist_ref[...] = prev[prev.shape[0] - CONV_HIST:, :]

    x = x_ref[...]
    xn = _rms(x, n1_ref[...]).astype(BF16)
    h1 = x + 0.5 * _ffn(xn, wi_ref, wo_ref)
    h_ref[...] = h1
    u = _rms(h1, nm_ref[...]).astype(BF16)
    tail = _dot(u, win_ref[:, _C_QA:_C_END])
    misc = tail[:, _C_DT - _C_QA:]
    dt_ref[...] = misc
    qn = _rms(tail[:, :_C_KVA - _C_QA], qn_ref[...]).astype(BF16)
    kvn = _rms(tail[:, _C_KVA - _C_QA:_C_DT - _C_QA], kn_ref[...]).astype(BF16)
    c, s_lo, s_hi = c_ref[...], slo_ref[...], shi_ref[...]
    lane = lax.broadcasted_iota(jnp.int32, (1, ROPE_TILE), 1)
    first_group = lane < QK_ROPE_DIM
    kr = jnp.where(first_group, pltpu.roll(_rotate(misc, c, s_lo, s_hi), shift=QK_ROPE_DIM, axis=1),
                   0.0).astype(BF16)
    q_scale = SOFTMAX_SCALE * LOG2_E
    k_nope = _dot(kvn, wk_ref[...])
    for h in range(MLA_HEADS):
        nope = slice(h * QK_NOPE_DIM, (h + 1) * QK_NOPE_DIM)
        k_ref[:, h * QK_PAD:h * QK_PAD + QK_NOPE_DIM] = k_nope[:, nope].astype(BF16)
        k_ref[:, h * QK_PAD + QK_NOPE_DIM:(h + 1) * QK_PAD] = kr
    nt = (((1,), (1,)), ((), ()))
    qt_nope = lax.dot_general(wqn_ref[...], qn, nt, preferred_element_type=F32) * q_scale
    qt_rope = lax.dot_general(wqr_ref[...], qn, nt, preferred_element_type=F32) * q_scale
    cos_t = c.T[:ROPE_HALF, :]
    sin_t = s_hi.T[ROPE_HALF:QK_ROPE_DIM, :]
    zero_rows = jnp.zeros((QK_PAD - QK_NOPE_DIM - QK_ROPE_DIM, qn.shape[0]), BF16)
    for h in range(MLA_HEADS):
        x1 = qt_rope[h * QK_ROPE_DIM:h * QK_ROPE_DIM + ROPE_HALF, :]
        x2 = qt_rope[h * QK_ROPE_DIM + ROPE_HALF:(h + 1) * QK_ROPE_DIM, :]
        base = h * QK_PAD
        q_ref[0, base:base + QK_NOPE_DIM, :] = qt_nope[h * QK_NOPE_DIM:(h + 1) * QK_NOPE_DIM, :].astype(BF16)
        q_ref[0, base + QK_NOPE_DIM:base + QK_NOPE_DIM + ROPE_HALF, :] = (x1 * cos_t - x2 * sin_t).astype(BF16)
        q_ref[0, base + QK_NOPE_DIM + ROPE_HALF:base + QK_NOPE_DIM + QK_ROPE_DIM, :] = (
            x2 * cos_t + x1 * sin_t).astype(BF16)
        q_ref[0, base + QK_NOPE_DIM + QK_ROPE_DIM:base + QK_PAD, :] = zero_rows
    v_ref[0] = lax.dot_general(wv_ref[...], kvn, nt, preferred_element_type=F32).astype(BF16)
    zg_ref[...] = _silu(_dot(u, win_ref[:, _C_Z:_C_XBC]))
    raw_ref[...] = _dot(u, win_ref[:, _C_XBC:_C_QA])


def _pre_call(h, params, rope, seq, layer):
    t = h.shape[0]
    tm = ATTN_TK
    n = t // tm
    cur = lambda i: (jnp.minimum(i, n - 1), 0)
    prev = lambda i: (jnp.maximum(i - 1, 0), 0)
    in_specs = ([pl.BlockSpec((tm, D_MODEL), cur)] + _param_specs(params, 1, layer)
                + [pl.BlockSpec((tm, ROPE_TILE), cur)] * len(rope))
    rows = lambda w, d, im: (pl.BlockSpec((tm, w), im), jax.ShapeDtypeStruct((t, w), d))
    cols = lambda w: (pl.BlockSpec((1, w, tm), lambda i: (jnp.minimum(i, n - 1), 0, 0)),
                      jax.ShapeDtypeStruct((n, w, tm), BF16))
    outs = [rows(D_MODEL, F32, cur), rows(D_SSM, F32, cur), rows(CONV_DIM, F32, prev),
            rows(DT_PAD, F32, cur), cols(MLA_HEADS * QK_PAD), rows(MLA_HEADS * QK_PAD, BF16, cur),
            cols(D_MLA)]
    return pl.pallas_call(
        functools.partial(_pre_kernel, tiles_per_seq=seq // tm),
        grid=(n + 1,),
        in_specs=in_specs,
        out_specs=[spec for spec, _ in outs],
        out_shape=[shape for _, shape in outs],
        scratch_shapes=[pltpu.VMEM((tm, CONV_DIM), F32), pltpu.VMEM((CONV_HIST, CONV_DIM), F32)],
        compiler_params=pltpu.CompilerParams(
            dimension_semantics=("arbitrary",), vmem_limit_bytes=V7X_VMEM_LIMIT_BYTES),
        name="pre",
    )(h, *params, *rope)


def _split_dot(v, m, parts, left=False):
    out = None
    rem = v
    for _ in range(parts):
        piece = rem.astype(BF16)
        rem = rem - piece.astype(F32)
        d = _dot(m, piece) if left else _dot(piece, m)
        out = d if out is None else out + d
    return out


def _softplus(x):
    return jnp.maximum(x, 0.0) + jnp.log1p(jnp.exp(-jnp.abs(x)))


def _ssd_block(xbc_ref, zg_ref, dt_ref, dtb_ref, alog_ref, dskip_ref, nw_ref,
               tri_ref, exp_ref, y_ref, state_ref):
    q = SSD_BLOCK
    hp = SSD_HEADS_PER_GROUP * SSD_HEAD_DIM
    xbc = xbc_ref[...]
    xs = xbc[:, :D_SSM]

    lane = lax.broadcasted_iota(jnp.int32, (1, DT_PAD), 1)
    head_lane = lane < SSD_HEADS
    dt = jnp.where(head_lane, _softplus(dt_ref[...] + dtb_ref[...]), 0.0)
    a_neg = jnp.where(head_lane, -jnp.exp(alog_ref[...]), 0.0)
    a_cum = _split_dot(dt * (a_neg * LOG2_E), tri_ref[...], 3, left=True)
    a_cum_t = a_cum.T
    a_last = a_cum[q - 1:q, :]
    to_end = jnp.exp2(a_last - a_cum)
    from_start = jnp.exp2(a_cum)

    def expand(v):
        hi = v.astype(BF16)
        lo = (v - hi.astype(F32)).astype(BF16)
        return _dot(jnp.concatenate([hi, lo], axis=1), exp_ref[...])

    dt_e = expand(dt)
    w_e = expand(dt * to_end)
    fs_e = expand(from_start)
    xdt = (xs * dt_e).astype(BF16)
    xw = (xs * w_e).astype(BF16)

    row = lax.broadcasted_iota(jnp.int32, (q, q), 0)
    col = lax.broadcasted_iota(jnp.int32, (q, q), 1)
    causal = col <= row
    lane_pair = lax.broadcasted_iota(jnp.int32, (1, 2 * SSD_HEAD_DIM), 1)
    first_of_pair = lane_pair < SSD_HEAD_DIM

    zg = zg_ref[...]
    for g in range(SSD_GROUPS):
        yield
        b_g = xbc[:, D_SSM + g * SSD_STATE:D_SSM + (g + 1) * SSD_STATE]
        c_g = xbc[:, D_SSM + (SSD_GROUPS + g) * SSD_STATE:D_SSM + (SSD_GROUPS + g + 1) * SSD_STATE]
        b_bf = b_g.astype(BF16)
        c_bf = c_g.astype(BF16)
        cb = lax.dot_general(c_bf, b_bf, (((1,), (1,)), ((), ())), preferred_element_type=F32)
        gsl = slice(g * hp, (g + 1) * hp)
        state = state_ref[g]
        y_g = _dot(c_bf, state.astype(BF16)) * fs_e[:, gsl] + xs[:, gsl] * dskip_ref[:, gsl]
        pieces = []
        for pair in range(SSD_HEADS_PER_GROUP // 2):
            outs = []
            for sub in range(2):
                h = g * SSD_HEADS_PER_GROUP + 2 * pair + sub
                seg = a_cum[:, h:h + 1] - a_cum_t[h:h + 1, :]
                m = (cb * jnp.exp2(jnp.where(causal, seg, NEG_BIG))).astype(BF16)
                psl = slice(g * hp + pair * 2 * SSD_HEAD_DIM, g * hp + (pair + 1) * 2 * SSD_HEAD_DIM)
                outs.append(_dot(m, xdt[:, psl]))
            pieces.append(jnp.where(first_of_pair, outs[0], outs[1]))
        y_g = y_g + jnp.concatenate(pieces, axis=-1)
        y_g = y_g * zg[:, gsl]
        y_g = y_g * lax.rsqrt(jnp.mean(y_g * y_g, axis=-1, keepdims=True) + EPS) * nw_ref[:, gsl]
        y_ref[:, gsl] = y_g.astype(y_ref.dtype)
        state_ref[g] = state * fs_e[q - 1:q, gsl] + _dot(b_g.T.astype(BF16), xw[:, gsl])


def _mix_kernel(q_ref, k_ref, vt_ref, xbc_ref, zg_ref, dt_ref, dtb_ref, alog_ref, dskip_ref, nw_ref,
                tri_ref, exp_ref, o_ref, y_ref, m_ref, l_ref, acc_ref, state_ref):
    @pl.when((pl.program_id(1) == 0) & (pl.program_id(2) == 0))
    def _():
        state_ref[...] = jnp.zeros_like(state_ref)

    ssd = _ssd_block(xbc_ref, zg_ref, dt_ref, dtb_ref, alog_ref, dskip_ref, nw_ref, tri_ref, exp_ref,
                     y_ref, state_ref)
    _attn_body(q_ref, k_ref, vt_ref, o_ref, m_ref, l_ref, acc_ref, ssd)


def _attn_body(q_ref, k_ref, vt_ref, o_ref, m_ref, l_ref, acc_ref, side_work):
    tk = ATTN_TK
    tq = q_ref.shape[0] * tk
    qi = pl.program_id(2)
    heads = range(ATTN_HEADS_PER_STEP)
    ones = jnp.ones((ATTN_ONES_ROWS, tk), BF16)

    def qk(g, j, tiles, q_lo):
        start = pl.multiple_of(j * tk, tk)
        k = k_ref[pl.ds(start, tiles * tk), g * QK_PAD:(g + 1) * QK_PAD]
        qt = jnp.concatenate([q_ref[t, g * QK_PAD:(g + 1) * QK_PAD, :] for t in range(q_lo // tk, tq // tk)],
                             axis=1)
        return _dot(k, qt)

    def fold(g, s, j, tiles, q_lo, mask, first):
        if mask is not None:
            s = jnp.where(mask, s, NEG_BIG)
        m_blk = jnp.max(s, axis=0, keepdims=True)
        if first:
            m_new = m_blk
        else:
            m_old = m_ref[g, :, q_lo:]
            m_new = jnp.maximum(m_old, m_blk)
            alpha = jnp.exp2(m_old - m_new)
        p = jnp.exp2(s - m_new).astype(BF16)
        pv = None
        for i in range(tiles):
            vt = jnp.concatenate([vt_ref[j + i, g * V_HEAD_DIM:(g + 1) * V_HEAD_DIM, :], ones], axis=0)
            d = _dot(vt, p[i * tk:(i + 1) * tk, :])
            pv = d if pv is None else pv + d
        p_sum = pv[V_HEAD_DIM:V_HEAD_DIM + 1, :]
        pv = pv[:V_HEAD_DIM, :]
        if first:
            l_ref[g, :, q_lo:] = p_sum
            acc_ref[g, :, q_lo:] = pv
        else:
            l_ref[g, :, q_lo:] = alpha * l_ref[g, :, q_lo:] + p_sum
            acc_ref[g, :, q_lo:] = alpha * acc_ref[g, :, q_lo:] + pv
        m_ref[g, :, q_lo:] = m_new

    def sweep(j, tiles, q_lo, mask, first):
        scores = [qk(g, j, tiles, q_lo) for g in heads]
        for g in heads:
            fold(g, scores[g], j, tiles, q_lo, mask, first)

    for d in range(tq // tk):
        q_lo = d * tk
        key = lax.broadcasted_iota(jnp.int32, (tk, tq - q_lo), 0)
        query = lax.broadcasted_iota(jnp.int32, (tk, tq - q_lo), 1)
        next(side_work, None)
        sweep(qi * (tq // tk) + d, 1, q_lo, key // CHUNK <= query // CHUNK, first=(d == 0))
    for _ in side_work:
        pass

    sweeps = qi * (tq // tk) // ATTN_LOOP_TILES
    odd = sweeps % 2

    @pl.when(odd == 1)
    def _():
        sweep(0, ATTN_LOOP_TILES, 0, None, first=False)

    @pl.loop(0, sweeps // 2)
    def _(trip):
        j = (odd + 2 * trip) * ATTN_LOOP_TILES
        sweep(j, ATTN_LOOP_TILES, 0, None, first=False)
        sweep(j + ATTN_LOOP_TILES, ATTN_LOOP_TILES, 0, None, first=False)

    for g in heads:
        o_ref[:, g * V_HEAD_DIM:(g + 1) * V_HEAD_DIM] = (acc_ref[g] / l_ref[g]).T.astype(o_ref.dtype)


def _mix_call(q, k, vt, xbc, zg, dtm, params, batch, seq, layer):
    tq = min(seq, ATTN_TQ)
    nq = seq // tq
    g = ATTN_HEADS_PER_STEP
    groups = MLA_HEADS // g
    nblk = seq // SSD_BLOCK
    hp = SSD_HEADS_PER_GROUP * SSD_HEAD_DIM
    assert (tq // ATTN_TK) % ATTN_LOOP_TILES == 0 and seq % tq == 0 and MLA_HEADS % g == 0
    assert groups * nq == nblk
    ssd_blk = lambda w: pl.BlockSpec((SSD_BLOCK, w), lambda b, h, i: (b * nblk + h * nq + i, 0))
    return pl.pallas_call(
        _mix_kernel,
        grid=(batch, groups, nq),
        in_specs=[pl.BlockSpec((tq // ATTN_TK, g * QK_PAD, ATTN_TK), lambda b, h, i: (b * nq + i, h, 0)),
                  pl.BlockSpec((seq, g * QK_PAD), lambda b, h, i: (b, h)),
                  pl.BlockSpec((seq // ATTN_TK, g * V_HEAD_DIM, ATTN_TK), lambda b, h, i: (b, h, 0)),
                  ssd_blk(CONV_DIM), ssd_blk(D_SSM), ssd_blk(DT_PAD)] + _param_specs(params, 3, layer),
        out_specs=[pl.BlockSpec((tq, g * V_HEAD_DIM), lambda b, h, i: (b * nq + i, h)), ssd_blk(D_SSM)],
        out_shape=[jax.ShapeDtypeStruct((batch * seq, D_MLA), BF16),
                   jax.ShapeDtypeStruct((batch * seq, D_SSM), BF16)],
        scratch_shapes=[pltpu.VMEM((g, 1, tq), F32), pltpu.VMEM((g, 1, tq), F32),
                        pltpu.VMEM((g, V_HEAD_DIM, tq), F32),
                        pltpu.VMEM((SSD_GROUPS, SSD_STATE, hp), F32)],
        compiler_params=pltpu.CompilerParams(
            dimension_semantics=("arbitrary", "arbitrary", "arbitrary"),
            vmem_limit_bytes=V7X_VMEM_LIMIT_BYTES),
        name="mix",
    )(q, k, vt, xbc, zg, dtm, *params)


def _post_kernel(h_ref, ys_ref, ym_ref, p_ref, wmix_ref, n2_ref, wi_ref, wo_ref,
                 np_ref, wgate_ref, wproj_ref, nf_ref, o_ref, *, final):
    h2 = (h_ref[...] + _dot(ys_ref[...], wmix_ref[:D_SSM, :]) + _dot(ym_ref[...], wmix_ref[D_SSM:, :]))
    xn = _rms(h2, n2_ref[...]).astype(BF16)
    h3 = h2 + 0.5 * _ffn(xn, wi_ref, wo_ref)
    gate = jax.nn.sigmoid(_dot(_rms(h3, np_ref[...]).astype(BF16), wgate_ref[...]))
    h4 = h3 + gate * _dot(p_ref[...].astype(BF16), wproj_ref[...])
    if final:
        h4 = _rms(h4, nf_ref[...])
    o_ref[...] = h4


def _post_call(h, ys, ym, p, params, layer, final):
    t = h.shape[0]
    tm = min(t, TOKEN_TILE)
    in_specs = ([_row_spec(tm, D_MODEL), _row_spec(tm, D_SSM), _row_spec(tm, D_MLA),
                 pl.BlockSpec((None, tm, PLE_DIM), lambda i: (layer, i, 0))]
                + _param_specs(params, 1, layer))
    return pl.pallas_call(
        functools.partial(_post_kernel, final=final),
        grid=(t // tm,),
        in_specs=in_specs,
        out_specs=_row_spec(tm, D_MODEL),
        out_shape=jax.ShapeDtypeStruct((t, D_MODEL), F32),
        compiler_params=pltpu.CompilerParams(
            dimension_semantics=("parallel",), vmem_limit_bytes=V7X_VMEM_LIMIT_BYTES),
        name="post",
    )(h, ys, ym, p, *params)


def _pad_last(w, width):
    return jnp.pad(w, [(0, 0)] * (w.ndim - 1) + [(0, width - w.shape[-1])])


def _layout_in_proj(w):
    o = 0
    w_z = w[..., o:o + D_SSM]; o += D_SSM
    w_xbc = w[..., o:o + CONV_DIM]; o += CONV_DIM
    w_dt = w[..., o:o + SSD_HEADS]; o += SSD_HEADS
    w_qa = w[..., o:o + Q_LORA_RANK]; o += Q_LORA_RANK
    w_kva = w[..., o:o + KV_LORA_RANK]; o += KV_LORA_RANK
    w_kr = w[..., o:o + QK_ROPE_DIM]
    return jnp.concatenate([
        w_z, w_xbc, w_qa, w_kva, _pad_last(w_dt, DT_PAD - QK_ROPE_DIM), w_kr], axis=-1).astype(BF16)


def _layout_q_b(w):
    depth = w.shape[0]
    w = jnp.swapaxes(w, 1, 2).reshape(depth, MLA_HEADS, QK_NOPE_DIM + QK_ROPE_DIM, Q_LORA_RANK)
    return (w[:, :, :QK_NOPE_DIM].reshape(depth, MLA_HEADS * QK_NOPE_DIM, Q_LORA_RANK).astype(BF16),
            w[:, :, QK_NOPE_DIM:].reshape(depth, MLA_HEADS * QK_ROPE_DIM, Q_LORA_RANK).astype(BF16))


def _layout_kv_b(w):
    depth = w.shape[0]
    w = w.reshape(depth, KV_LORA_RANK, MLA_HEADS, QK_NOPE_DIM + V_HEAD_DIM)
    wk = w[..., :QK_NOPE_DIM].reshape(depth, KV_LORA_RANK, MLA_HEADS * QK_NOPE_DIM)
    wv_t = jnp.swapaxes(w[..., QK_NOPE_DIM:].reshape(depth, KV_LORA_RANK, D_MLA), 1, 2)
    return wk.astype(BF16), wv_t.astype(BF16)


def _rows(v):
    return v.reshape(v.shape[0], 1, -1).astype(F32)


def kernel(x, p, positions, ffn1_norm, ffn1_w_in, ffn1_w_out, mix_norm, w_in_mix, conv_w, conv_b,
           dt_bias, a_log, d_skip, ssd_norm, q_a_norm, w_q_b, kv_a_norm, w_kv_b, w_out_mix,
           ffn2_norm, ffn2_w_in, ffn2_w_out, ple_norm, w_ple_gate, w_ple_proj, final_norm):
    batch, seq, _ = x.shape
    depth = p.shape[0]
    t = batch * seq
    assert seq % SSD_BLOCK == 0 and seq % min(seq, ATTN_TQ) == 0 and t % TOKEN_TILE == 0

    rope = _rope_tables(positions)
    q_idx = jnp.arange(SSD_BLOCK)
    tri = (q_idx[None, :] <= q_idx[:, None]).astype(BF16)
    expand = (jnp.arange(D_SSM)[None, :] // SSD_HEAD_DIM == jnp.arange(DT_PAD)[:, None]).astype(BF16)
    expand = jnp.concatenate([expand, expand], axis=0)

    wqn, wqr = _layout_q_b(w_q_b)
    wk, wv = _layout_kv_b(w_kv_b)
    pre_params = [_rows(ffn1_norm), ffn1_w_in.astype(BF16), ffn1_w_out.astype(BF16), _rows(mix_norm),
                  _layout_in_proj(w_in_mix), _rows(q_a_norm), _rows(kv_a_norm), wqn, wqr, wk, wv,
                  conv_w.astype(F32), _rows(conv_b)]
    ssd_params = [_pad_last(_rows(dt_bias), DT_PAD), _pad_last(_rows(a_log), DT_PAD),
                  _rows(jnp.repeat(d_skip, SSD_HEAD_DIM, axis=-1)), _rows(ssd_norm), tri, expand]
    post_params = [w_out_mix.astype(BF16), _rows(ffn2_norm), ffn2_w_in.astype(BF16),
                   ffn2_w_out.astype(BF16), _rows(ple_norm), w_ple_gate.astype(BF16),
                   w_ple_proj.astype(BF16), final_norm.reshape(1, -1).astype(F32)]
    p = p.reshape(depth, t, PLE_DIM)

    h = x.reshape(t, D_MODEL)
    for i in range(depth):
        h, zg, xbc, dtm, q, k, v = _pre_call(h, pre_params, rope, seq, i)
        y_mla, y_ssd = _mix_call(q, k, v, xbc, zg, dtm, ssd_params, batch, seq, i)
        h = _post_call(h, y_ssd, y_mla, p, post_params, i, final=(i == depth - 1))
    return h.reshape(batch, seq, D_MODEL)
```

```python
import functools

import jax
import jax.numpy as jnp
from jax import lax
from jax.experimental import pallas as pl
from jax.experimental.pallas import tpu as pltpu

F32 = jnp.float32
BF16 = jnp.bfloat16

D_MODEL = 1024
CHUNK = 64
PLE_DIM = 256
D_FF = 2816
EPS = 1e-6
SSD_HEADS = 16
SSD_HEAD_DIM = 64
D_SSM = SSD_HEADS * SSD_HEAD_DIM
SSD_GROUPS = 2
SSD_HEADS_PER_GROUP = SSD_HEADS // SSD_GROUPS
SSD_STATE = 128
CONV_WIDTH = 4
CONV_DIM = D_SSM + 2 * SSD_GROUPS * SSD_STATE
MLA_HEADS = 8
QK_NOPE_DIM = 128
QK_ROPE_DIM = 64
V_HEAD_DIM = 128
Q_LORA_RANK = 384
KV_LORA_RANK = 256
D_MLA = MLA_HEADS * V_HEAD_DIM
ROPE_THETA = 10000.0
SOFTMAX_SCALE = (QK_NOPE_DIM + QK_ROPE_DIM) ** -0.5
LOG2_E = 1.4426950408889634

V7X_LANES = 128
V7X_SUBLANES = 8
V7X_MXU_DIM = 256
V7X_VMEM_LIMIT_BYTES = 56 * 1024 * 1024

QK_PAD = V7X_MXU_DIM
ROPE_HALF = QK_ROPE_DIM // 2
ROPE_TILE = V7X_LANES
DT_PAD = V7X_LANES
FF_CHUNK = V7X_MXU_DIM
SSD_BLOCK = 256
CONV_HIST = V7X_SUBLANES
TOKEN_TILE = 512
ATTN_TQ = 512
ATTN_TK = 256
ATTN_HEADS_PER_STEP = 4
ATTN_LOOP_TILES = 2
ATTN_ONES_ROWS = 2 * V7X_SUBLANES
ATTN_SCORE_LEAD = 2
NEG_BIG = -1e30

_C_Z = 0
_C_XBC = _C_Z + D_SSM
_C_QA = _C_XBC + CONV_DIM
_C_KVA = _C_QA + Q_LORA_RANK
_C_DT = _C_KVA + KV_LORA_RANK
_C_END = _C_DT + DT_PAD


def _dot(a, b):
    return jnp.dot(a, b, preferred_element_type=F32)


def _rms(x, w):
    return x * lax.rsqrt(jnp.mean(x * x, axis=-1, keepdims=True) + EPS) * w


def _ffn(xn, wi_ref, wo_ref):
    n = D_FF // FF_CHUNK
    cols = lambda c: slice(c * FF_CHUNK, (c + 1) * FF_CHUNK)
    up_cols = lambda c: slice(D_FF + c * FF_CHUNK, D_FF + (c + 1) * FF_CHUNK)
    up = lambda c: (_dot(xn, wi_ref[:, cols(c)]), _dot(xn, wi_ref[:, up_cols(c)]))
    acc = None
    gu = up(0)
    for c in range(n):
        gu_next = up(c + 1) if c + 1 < n else None
        g, u = gu
        a = (g * jax.nn.sigmoid(g) * u).astype(BF16)
        d = _dot(a, wo_ref[cols(c), :])
        acc = d if acc is None else acc + d
        gu = gu_next
    return acc


def _const_spec(shape, grid_rank, layer=None):
    index = (0,) * len(shape) if layer is None else (layer,) + (0,) * (len(shape) - 1)
    block = tuple(shape) if layer is None else (None,) + tuple(shape[1:])
    if grid_rank == 1:
        index_map = lambda i: index
    elif grid_rank == 2:
        index_map = lambda i, j: index
    else:
        index_map = lambda i, j, k: index
    return pl.BlockSpec(block, index_map, pipeline_mode=pl.Buffered(1))


def _param_specs(params, grid_rank, layer):
    return [_const_spec(a.shape, grid_rank, layer if a.ndim == 3 else None) for a in params]


def _row_spec(tm, width):
    return pl.BlockSpec((tm, width), lambda i: (i, 0))


def _rope_kernel(pos_ref, inv_ref, lo_ref, hi_ref, c_ref, slo_ref, shi_ref):
    ang = pos_ref[...].astype(F32) * inv_ref[...]
    sin = jnp.sin(ang)
    c_ref[...] = jnp.cos(ang)
    slo_ref[...] = sin * lo_ref[...]
    shi_ref[...] = sin * hi_ref[...]


def _rope_tables(positions):
    t = positions.size
    tm = min(t, 1024)
    groups = ROPE_TILE // QK_ROPE_DIM
    inv = ROPE_THETA ** (-jnp.arange(0, QK_ROPE_DIM, 2, dtype=F32) / QK_ROPE_DIM)
    ones_h = jnp.ones((ROPE_HALF,), F32)
    zeros_h = jnp.zeros((ROPE_HALF,), F32)
    inv_row = jnp.tile(inv, 2 * groups)[None, :]
    lo = jnp.tile(jnp.concatenate([-ones_h, zeros_h]), groups)[None, :]
    hi = jnp.tile(jnp.concatenate([zeros_h, ones_h]), groups)[None, :]
    row = _const_spec((1, ROPE_TILE), 1)
    return pl.pallas_call(
        _rope_kernel,
        grid=(t // tm,),
        in_specs=[pl.BlockSpec((tm, 1), lambda i: (i, 0)), row, row, row],
        out_specs=[_row_spec(tm, ROPE_TILE)] * 3,
        out_shape=[jax.ShapeDtypeStruct((t, ROPE_TILE), F32)] * 3,
        compiler_params=pltpu.CompilerParams(dimension_semantics=("parallel",)),
        name="rope_tables",
    )(positions.reshape(t, 1), inv_row, lo, hi)


def _rotate(x, c, s_lo, s_hi):
    ahead = pltpu.roll(x, shift=ROPE_TILE - ROPE_HALF, axis=1)
    behind = pltpu.roll(x, shift=ROPE_HALF, axis=1)
    return x * c + ahead * s_lo + behind * s_hi


def _silu(x):
    return x * jax.nn.sigmoid(x)


def _pre_kernel(x_ref, n1_ref, wi_ref, wo_ref, nm_ref, win_ref, qn_ref, kn_ref,
                wqn_ref, wqr_ref, wk_ref, wv_ref, cw_ref, cb_ref, c_ref, slo_ref, shi_ref,
                h_ref, zg_ref, xbc_ref, dt_ref, q_ref, k_ref, v_ref, raw_ref, hist_ref,
                *, tiles_per_seq):
    i = pl.program_id(0)

    @pl.when(i == 0)
    def _():
        raw_ref[...] = jnp.zeros_like(raw_ref)
        hist_ref[...] = jnp.zeros_like(hist_ref)

    prev = raw_ref[...]
    seq_start = lax.rem(i - 1, tiles_per_seq) == 0
    hist = jnp.where(seq_start, 0.0, hist_ref[...])
    xpad = jnp.concatenate([hist, prev], axis=0)
    conv = cb_ref[...] + cw_ref[CONV_WIDTH - 1:CONV_WIDTH, :] * prev
    for back in range(1, CONV_WIDTH):
        shifted = pltpu.roll(xpad, shift=back, axis=0)[CONV_HIST:, :]
        conv = conv + cw_ref[CONV_WIDTH - 1 - back:CONV_WIDTH - back, :] * shifted
    xbc_ref[...] = _silu(conv)
    hist_ref[...] = prev[prev.shape[0] - CONV_HIST:, :]

    x = x_ref[...]
    xn = _rms(x, n1_ref[...]).astype(BF16)
    h1 = x + 0.5 * _ffn(xn, wi_ref, wo_ref)
    h_ref[...] = h1
    u = _rms(h1, nm_ref[...]).astype(BF16)
    tail = _dot(u, win_ref[:, _C_QA:_C_END])
    misc = tail[:, _C_DT - _C_QA:]
    dt_ref[...] = misc
    qn = _rms(tail[:, :_C_KVA - _C_QA], qn_ref[...]).astype(BF16)
    kvn = _rms(tail[:, _C_KVA - _C_QA:_C_DT - _C_QA], kn_ref[...]).astype(BF16)
    c, s_lo, s_hi = c_ref[...], slo_ref[...], shi_ref[...]
    lane = lax.broadcasted_iota(jnp.int32, (1, ROPE_TILE), 1)
    first_group = lane < QK_ROPE_DIM
    kr = jnp.where(first_group, pltpu.roll(_rotate(misc, c, s_lo, s_hi), shift=QK_ROPE_DIM, axis=1),
                   0.0).astype(BF16)
    q_scale = SOFTMAX_SCALE * LOG2_E
    q_nope = _dot(qn, wqn_ref[...]) * q_scale
    k_nope = _dot(kvn, wk_ref[...])
    q_rope = _dot(qn, wqr_ref[...])
    for pair in range(MLA_HEADS // 2):
        tile = slice(pair * ROPE_TILE, (pair + 1) * ROPE_TILE)
        rot = _rotate(q_rope[:, tile], c, s_lo, s_hi) * q_scale
        for sub, part in enumerate((jnp.where(first_group, rot, 0.0),
                                    jnp.where(first_group, pltpu.roll(rot, shift=QK_ROPE_DIM, axis=1), 0.0))):
            base = (2 * pair + sub) * QK_PAD
            q_ref[:, base + QK_NOPE_DIM:base + QK_PAD] = part.astype(BF16)
    for h in range(MLA_HEADS):
        nope = slice(h * QK_NOPE_DIM, (h + 1) * QK_NOPE_DIM)
        q_ref[:, h * QK_PAD:h * QK_PAD + QK_NOPE_DIM] = q_nope[:, nope].astype(BF16)
        k_ref[:, h * QK_PAD:h * QK_PAD + QK_NOPE_DIM] = k_nope[:, nope].astype(BF16)
        k_ref[:, h * QK_PAD + QK_NOPE_DIM:(h + 1) * QK_PAD] = kr
    v_ref[0] = lax.dot_general(wv_ref[...], kvn, (((1,), (1,)), ((), ())),
                               preferred_element_type=F32).astype(BF16)
    zg_ref[...] = _silu(_dot(u, win_ref[:, _C_Z:_C_XBC]))
    raw_ref[...] = _dot(u, win_ref[:, _C_XBC:_C_QA])


def _pre_call(h, params, rope, seq, layer):
    t = h.shape[0]
    tm = ATTN_TK
    n = t // tm
    cur = lambda i: (jnp.minimum(i, n - 1), 0)
    prev = lambda i: (jnp.maximum(i - 1, 0), 0)
    in_specs = ([pl.BlockSpec((tm, D_MODEL), cur)] + _param_specs(params, 1, layer)
                + [pl.BlockSpec((tm, ROPE_TILE), cur)] * len(rope))
    outs = [(D_MODEL, F32, cur), (D_SSM, F32, cur), (CONV_DIM, F32, prev), (DT_PAD, F32, cur),
            (MLA_HEADS * QK_PAD, BF16, cur), (MLA_HEADS * QK_PAD, BF16, cur)]
    return pl.pallas_call(
        functools.partial(_pre_kernel, tiles_per_seq=seq // tm),
        grid=(n + 1,),
        in_specs=in_specs,
        out_specs=[pl.BlockSpec((tm, w), im) for w, _, im in outs]
        + [pl.BlockSpec((1, D_MLA, tm), lambda i: (jnp.minimum(i, n - 1), 0, 0))],
        out_shape=[jax.ShapeDtypeStruct((t, w), d) for w, d, _ in outs]
        + [jax.ShapeDtypeStruct((n, D_MLA, tm), BF16)],
        scratch_shapes=[pltpu.VMEM((tm, CONV_DIM), F32), pltpu.VMEM((CONV_HIST, CONV_DIM), F32)],
        compiler_params=pltpu.CompilerParams(
            dimension_semantics=("arbitrary",), vmem_limit_bytes=V7X_VMEM_LIMIT_BYTES),
        name="pre",
    )(h, *params, *rope)


def _split_dot(v, m, parts, left=False):
    out = None
    rem = v
    for _ in range(parts):
        piece = rem.astype(BF16)
        rem = rem - piece.astype(F32)
        d = _dot(m, piece) if left else _dot(piece, m)
        out = d if out is None else out + d
    return out


def _softplus(x):
    return jnp.maximum(x, 0.0) + jnp.log1p(jnp.exp(-jnp.abs(x)))


def _ssd_block(xbc_ref, zg_ref, dt_ref, dtb_ref, alog_ref, dskip_ref, nw_ref,
               tri_ref, exp_ref, y_ref, state_ref):
    q = SSD_BLOCK
    hp = SSD_HEADS_PER_GROUP * SSD_HEAD_DIM
    xbc = xbc_ref[...]
    xs = xbc[:, :D_SSM]

    lane = lax.broadcasted_iota(jnp.int32, (1, DT_PAD), 1)
    head_lane = lane < SSD_HEADS
    dt = jnp.where(head_lane, _softplus(dt_ref[...] + dtb_ref[...]), 0.0)
    a_neg = jnp.where(head_lane, -jnp.exp(alog_ref[...]), 0.0)
    a_cum = _split_dot(dt * (a_neg * LOG2_E), tri_ref[...], 3, left=True)
    a_cum_t = a_cum.T
    a_last = a_cum[q - 1:q, :]
    to_end = jnp.exp2(a_last - a_cum)
    from_start = jnp.exp2(a_cum)

    def expand(v):
        hi = v.astype(BF16)
        lo = (v - hi.astype(F32)).astype(BF16)
        return _dot(jnp.concatenate([hi, lo], axis=1), exp_ref[...])

    dt_e = expand(dt)
    w_e = expand(dt * to_end)
    fs_e = expand(from_start)
    xdt = (xs * dt_e).astype(BF16)
    xw = (xs * w_e).astype(BF16)

    row = lax.broadcasted_iota(jnp.int32, (q, q), 0)
    col = lax.broadcasted_iota(jnp.int32, (q, q), 1)
    causal = col <= row
    lane_pair = lax.broadcasted_iota(jnp.int32, (1, 2 * SSD_HEAD_DIM), 1)
    first_of_pair = lane_pair < SSD_HEAD_DIM

    zg = zg_ref[...]
    for g in range(SSD_GROUPS):
        yield
        b_g = xbc[:, D_SSM + g * SSD_STATE:D_SSM + (g + 1) * SSD_STATE]
        c_g = xbc[:, D_SSM + (SSD_GROUPS + g) * SSD_STATE:D_SSM + (SSD_GROUPS + g + 1) * SSD_STATE]
        b_bf = b_g.astype(BF16)
        c_bf = c_g.astype(BF16)
        cb = lax.dot_general(c_bf, b_bf, (((1,), (1,)), ((), ())), preferred_element_type=F32)
        gsl = slice(g * hp, (g + 1) * hp)
        state = state_ref[g]
        y_g = _dot(c_bf, state.astype(BF16)) * fs_e[:, gsl] + xs[:, gsl] * dskip_ref[:, gsl]
        pieces = []
        for pair in range(SSD_HEADS_PER_GROUP // 2):
            outs = []
            for sub in range(2):
                h = g * SSD_HEADS_PER_GROUP + 2 * pair + sub
                seg = a_cum[:, h:h + 1] - a_cum_t[h:h + 1, :]
                m = (cb * jnp.exp2(jnp.where(causal, seg, NEG_BIG))).astype(BF16)
                psl = slice(g * hp + pair * 2 * SSD_HEAD_DIM, g * hp + (pair + 1) * 2 * SSD_HEAD_DIM)
                outs.append(_dot(m, xdt[:, psl]))
            pieces.append(jnp.where(first_of_pair, outs[0], outs[1]))
        y_g = y_g + jnp.concatenate(pieces, axis=-1)
        y_g = y_g * zg[:, gsl]
        y_g = y_g * lax.rsqrt(jnp.mean(y_g * y_g, axis=-1, keepdims=True) + EPS) * nw_ref[:, gsl]
        y_ref[:, gsl] = y_g.astype(y_ref.dtype)
        state_ref[g] = state * fs_e[q - 1:q, gsl] + _dot(b_g.T.astype(BF16), xw[:, gsl])


def _mix_kernel(q_ref, k_ref, vt_ref, xbc_ref, zg_ref, dt_ref, dtb_ref, alog_ref, dskip_ref, nw_ref,
                tri_ref, exp_ref, o_ref, y_ref, m_ref, l_ref, acc_ref, state_ref):
    @pl.when((pl.program_id(1) == 0) & (pl.program_id(2) == 0))
    def _():
        state_ref[...] = jnp.zeros_like(state_ref)

    ssd = _ssd_block(xbc_ref, zg_ref, dt_ref, dtb_ref, alog_ref, dskip_ref, nw_ref, tri_ref, exp_ref,
                     y_ref, state_ref)
    _attn_body(q_ref, k_ref, vt_ref, o_ref, m_ref, l_ref, acc_ref, ssd)


def _attn_body(q_ref, k_ref, vt_ref, o_ref, m_ref, l_ref, acc_ref, side_work):
    tq, tk = q_ref.shape[0], ATTN_TK
    qi = pl.program_id(2)
    heads = range(ATTN_HEADS_PER_STEP)
    ones = jnp.ones((ATTN_ONES_ROWS, tk), BF16)

    def qk(g, j, tiles, q_lo):
        start = pl.multiple_of(j * tk, tk)
        k = k_ref[pl.ds(start, tiles * tk), g * QK_PAD:(g + 1) * QK_PAD]
        q = q_ref[q_lo:, g * QK_PAD:(g + 1) * QK_PAD]
        return lax.dot_general(k, q, (((1,), (1,)), ((), ())), preferred_element_type=F32)

    def fold(g, s, j, tiles, q_lo, q_hi, mask, first):
        if mask is not None:
            s = jnp.where(mask, s, NEG_BIG)
        m_blk = jnp.max(s, axis=0, keepdims=True)
        if first:
            m_new = m_blk
        else:
            m_old = m_ref[g, :, q_lo:q_hi]
            m_new = jnp.maximum(m_old, m_blk)
            alpha = jnp.exp2(m_old - m_new)
        p = jnp.exp2(s - m_new).astype(BF16)
        pv = None
        for i in range(tiles):
            vt = jnp.concatenate([vt_ref[j + i, g * V_HEAD_DIM:(g + 1) * V_HEAD_DIM, :], ones], axis=0)
            d = _dot(vt, p[i * tk:(i + 1) * tk, :])
            pv = d if pv is None else pv + d
        p_sum = pv[V_HEAD_DIM:V_HEAD_DIM + 1, :]
        pv = pv[:V_HEAD_DIM, :]
        if first:
            l_ref[g, :, q_lo:q_hi] = p_sum
            acc_ref[g, :, q_lo:q_hi] = pv
        else:
            l_ref[g, :, q_lo:q_hi] = alpha * l_ref[g, :, q_lo:q_hi] + p_sum
            acc_ref[g, :, q_lo:q_hi] = alpha * acc_ref[g, :, q_lo:q_hi] + pv
        m_ref[g, :, q_lo:q_hi] = m_new

    def sweep(j, tiles, q_lo, mask, first):
        ahead = [qk(g, j, tiles, q_lo) for g in heads[:ATTN_SCORE_LEAD]]
        for g in heads:
            if g + ATTN_SCORE_LEAD < len(heads):
                ahead.append(qk(g + ATTN_SCORE_LEAD, j, tiles, q_lo))
            fold(g, ahead.pop(0), j, tiles, q_lo, tq, mask, first)

    for d in range(tq // tk):
        q_lo = d * tk
        key = lax.broadcasted_iota(jnp.int32, (tk, tq - q_lo), 0)
        query = lax.broadcasted_iota(jnp.int32, (tk, tq - q_lo), 1)
        next(side_work, None)
        sweep(qi * (tq // tk) + d, 1, q_lo, key // CHUNK <= query // CHUNK, first=(d == 0))
    for _ in side_work:
        pass

    sweeps = qi * (tq // tk) // ATTN_LOOP_TILES
    odd = sweeps % 2

    @pl.when(odd == 1)
    def _():
        sweep(0, ATTN_LOOP_TILES, 0, None, first=False)

    @pl.loop(0, sweeps // 2)
    def _(trip):
        j = (odd + 2 * trip) * ATTN_LOOP_TILES
        sweep(j, ATTN_LOOP_TILES, 0, None, first=False)
        sweep(j + ATTN_LOOP_TILES, ATTN_LOOP_TILES, 0, None, first=False)

    for g in heads:
        o_ref[:, g * V_HEAD_DIM:(g + 1) * V_HEAD_DIM] = (acc_ref[g] / l_ref[g]).T.astype(o_ref.dtype)


def _mix_call(q, k, vt, xbc, zg, dtm, params, batch, seq, layer):
    tq = min(seq, ATTN_TQ)
    nq = seq // tq
    g = ATTN_HEADS_PER_STEP
    groups = MLA_HEADS // g
    nblk = seq // SSD_BLOCK
    hp = SSD_HEADS_PER_GROUP * SSD_HEAD_DIM
    assert (tq // ATTN_TK) % ATTN_LOOP_TILES == 0 and seq % tq == 0 and MLA_HEADS % g == 0
    assert groups * nq == nblk
    ssd_blk = lambda w: pl.BlockSpec((SSD_BLOCK, w), lambda b, h, i: (b * nblk + h * nq + i, 0))
    return pl.pallas_call(
        _mix_kernel,
        grid=(batch, groups, nq),
        in_specs=[pl.BlockSpec((tq, g * QK_PAD), lambda b, h, i: (b * nq + i, h)),
                  pl.BlockSpec((seq, g * QK_PAD), lambda b, h, i: (b, h)),
                  pl.BlockSpec((seq // ATTN_TK, g * V_HEAD_DIM, ATTN_TK), lambda b, h, i: (b, h, 0)),
                  ssd_blk(CONV_DIM), ssd_blk(D_SSM), ssd_blk(DT_PAD)] + _param_specs(params, 3, layer),
        out_specs=[pl.BlockSpec((tq, g * V_HEAD_DIM), lambda b, h, i: (b * nq + i, h)), ssd_blk(D_SSM)],
        out_shape=[jax.ShapeDtypeStruct((batch * seq, D_MLA), BF16),
                   jax.ShapeDtypeStruct((batch * seq, D_SSM), BF16)],
        scratch_shapes=[pltpu.VMEM((g, 1, tq), F32), pltpu.VMEM((g, 1, tq), F32),
                        pltpu.VMEM((g, V_HEAD_DIM, tq), F32),
                        pltpu.VMEM((SSD_GROUPS, SSD_STATE, hp), F32)],
        compiler_params=pltpu.CompilerParams(
            dimension_semantics=("arbitrary", "arbitrary", "arbitrary"),
            vmem_limit_bytes=V7X_VMEM_LIMIT_BYTES),
        name="mix",
    )(q, k, vt, xbc, zg, dtm, *params)


def _post_kernel(h_ref, ys_ref, ym_ref, p_ref, wmix_ref, n2_ref, wi_ref, wo_ref,
                 np_ref, wgate_ref, wproj_ref, nf_ref, o_ref, *, final):
    h2 = (h_ref[...] + _dot(ys_ref[...], wmix_ref[:D_SSM, :]) + _dot(ym_ref[...], wmix_ref[D_SSM:, :]))
    xn = _rms(h2, n2_ref[...]).astype(BF16)
    h3 = h2 + 0.5 * _ffn(xn, wi_ref, wo_ref)
    gate = jax.nn.sigmoid(_dot(_rms(h3, np_ref[...]).astype(BF16), wgate_ref[...]))
    h4 = h3 + gate * _dot(p_ref[...].astype(BF16), wproj_ref[...])
    if final:
        h4 = _rms(h4, nf_ref[...])
    o_ref[...] = h4


def _post_call(h, ys, ym, p, params, layer, final):
    t = h.shape[0]
    tm = min(t, TOKEN_TILE)
    in_specs = ([_row_spec(tm, D_MODEL), _row_spec(tm, D_SSM), _row_spec(tm, D_MLA),
                 pl.BlockSpec((None, tm, PLE_DIM), lambda i: (layer, i, 0))]
                + _param_specs(params, 1, layer))
    return pl.pallas_call(
        functools.partial(_post_kernel, final=final),
        grid=(t // tm,),
        in_specs=in_specs,
        out_specs=_row_spec(tm, D_MODEL),
        out_shape=jax.ShapeDtypeStruct((t, D_MODEL), F32),
        compiler_params=pltpu.CompilerParams(
            dimension_semantics=("parallel",), vmem_limit_bytes=V7X_VMEM_LIMIT_BYTES),
        name="post",
    )(h, ys, ym, p, *params)


def _pad_last(w, width):
    return jnp.pad(w, [(0, 0)] * (w.ndim - 1) + [(0, width - w.shape[-1])])


def _layout_in_proj(w):
    o = 0
    w_z = w[..., o:o + D_SSM]; o += D_SSM
    w_xbc = w[..., o:o + CONV_DIM]; o += CONV_DIM
    w_dt = w[..., o:o + SSD_HEADS]; o += SSD_HEADS
    w_qa = w[..., o:o + Q_LORA_RANK]; o += Q_LORA_RANK
    w_kva = w[..., o:o + KV_LORA_RANK]; o += KV_LORA_RANK
    w_kr = w[..., o:o + QK_ROPE_DIM]
    return jnp.concatenate([
        w_z, w_xbc, w_qa, w_kva, _pad_last(w_dt, DT_PAD - QK_ROPE_DIM), w_kr], axis=-1).astype(BF16)


def _layout_q_b(w):
    depth = w.shape[0]
    w = w.reshape(depth, Q_LORA_RANK, MLA_HEADS, QK_NOPE_DIM + QK_ROPE_DIM)
    return (w[..., :QK_NOPE_DIM].reshape(depth, Q_LORA_RANK, MLA_HEADS * QK_NOPE_DIM).astype(BF16),
            w[..., QK_NOPE_DIM:].reshape(depth, Q_LORA_RANK, MLA_HEADS * QK_ROPE_DIM).astype(BF16))


def _layout_kv_b(w):
    depth = w.shape[0]
    w = w.reshape(depth, KV_LORA_RANK, MLA_HEADS, QK_NOPE_DIM + V_HEAD_DIM)
    wk = w[..., :QK_NOPE_DIM].reshape(depth, KV_LORA_RANK, MLA_HEADS * QK_NOPE_DIM)
    wv_t = jnp.swapaxes(w[..., QK_NOPE_DIM:].reshape(depth, KV_LORA_RANK, D_MLA), 1, 2)
    return wk.astype(BF16), wv_t.astype(BF16)


def _rows(v):
    return v.reshape(v.shape[0], 1, -1).astype(F32)


def kernel(x, p, positions, ffn1_norm, ffn1_w_in, ffn1_w_out, mix_norm, w_in_mix, conv_w, conv_b,
           dt_bias, a_log, d_skip, ssd_norm, q_a_norm, w_q_b, kv_a_norm, w_kv_b, w_out_mix,
           ffn2_norm, ffn2_w_in, ffn2_w_out, ple_norm, w_ple_gate, w_ple_proj, final_norm):
    batch, seq, _ = x.shape
    depth = p.shape[0]
    t = batch * seq
    assert seq % SSD_BLOCK == 0 and seq % min(seq, ATTN_TQ) == 0 and t % TOKEN_TILE == 0

    rope = _rope_tables(positions)
    q_idx = jnp.arange(SSD_BLOCK)
    tri = (q_idx[None, :] <= q_idx[:, None]).astype(BF16)
    expand = (jnp.arange(D_SSM)[None, :] // SSD_HEAD_DIM == jnp.arange(DT_PAD)[:, None]).astype(BF16)
    expand = jnp.concatenate([expand, expand], axis=0)

    wqn, wqr = _layout_q_b(w_q_b)
    wk, wv = _layout_kv_b(w_kv_b)
    pre_params = [_rows(ffn1_norm), ffn1_w_in.astype(BF16), ffn1_w_out.astype(BF16), _rows(mix_norm),
                  _layout_in_proj(w_in_mix), _rows(q_a_norm), _rows(kv_a_norm), wqn, wqr, wk, wv,
                  conv_w.astype(F32), _rows(conv_b)]
    ssd_params = [_pad_last(_rows(dt_bias), DT_PAD), _pad_last(_rows(a_log), DT_PAD),
                  _rows(jnp.repeat(d_skip, SSD_HEAD_DIM, axis=-1)), _rows(ssd_norm), tri, expand]
    post_params = [w_out_mix.astype(BF16), _rows(ffn2_norm), ffn2_w_in.astype(BF16),
                   ffn2_w_out.astype(BF16), _rows(ple_norm), w_ple_gate.astype(BF16),
                   w_ple_proj.astype(BF16), final_norm.reshape(1, -1).astype(F32)]
    p = p.reshape(depth, t, PLE_DIM)

    h = x.reshape(t, D_MODEL)
    for i in range(depth):
        h, zg, xbc, dtm, q, k, v = _pre_call(h, pre_params, rope, seq, i)
        y_mla, y_ssd = _mix_call(q, k, v, xbc, zg, dtm, ssd_params, batch, seq, i)
        h = _post_call(h, y_ssd, y_mla, p, post_params, i, final=(i == depth - 1))
    return h.reshape(batch, seq, D_MODEL)
```

```python
import functools

import jax
import jax.numpy as jnp
from jax import lax
from jax.experimental import pallas as pl
from jax.experimental.pallas import tpu as pltpu

F32 = jnp.float32
BF16 = jnp.bfloat16

D_MODEL = 1024
CHUNK = 64
PLE_DIM = 256
D_FF = 2816
EPS = 1e-6
SSD_HEADS = 16
SSD_HEAD_DIM = 64
D_SSM = SSD_HEADS * SSD_HEAD_DIM
SSD_GROUPS = 2
SSD_HEADS_PER_GROUP = SSD_HEADS // SSD_GROUPS
SSD_STATE = 128
CONV_WIDTH = 4
CONV_DIM = D_SSM + 2 * SSD_GROUPS * SSD_STATE
MLA_HEADS = 8
QK_NOPE_DIM = 128
QK_ROPE_DIM = 64
V_HEAD_DIM = 128
Q_LORA_RANK = 384
KV_LORA_RANK = 256
D_MLA = MLA_HEADS * V_HEAD_DIM
ROPE_THETA = 10000.0
SOFTMAX_SCALE = (QK_NOPE_DIM + QK_ROPE_DIM) ** -0.5
LOG2_E = 1.4426950408889634

V7X_LANES = 128
V7X_SUBLANES = 8
V7X_MXU_DIM = 256
V7X_VMEM_LIMIT_BYTES = 56 * 1024 * 1024

QK_PAD = V7X_MXU_DIM
ROPE_HALF = QK_ROPE_DIM // 2
ROPE_TILE = V7X_LANES
DT_PAD = V7X_LANES
FF_CHUNK = V7X_MXU_DIM
SSD_BLOCK = 256
CONV_HIST = V7X_SUBLANES
TOKEN_TILE = 512
ATTN_TQ = 512
ATTN_TK = 256
ATTN_HEADS_PER_STEP = 4
ATTN_LOOP_TILES = 2
ATTN_ONES_ROWS = 2 * V7X_SUBLANES
NEG_BIG = -1e30

_C_Z = 0
_C_XBC = _C_Z + D_SSM
_C_QA = _C_XBC + CONV_DIM
_C_KVA = _C_QA + Q_LORA_RANK
_C_DT = _C_KVA + KV_LORA_RANK
_C_END = _C_DT + DT_PAD


def _dot(a, b):
    return jnp.dot(a, b, preferred_element_type=F32)


def _rms(x, w):
    return x * lax.rsqrt(jnp.mean(x * x, axis=-1, keepdims=True) + EPS) * w


def _ffn(xn, wi_ref, wo_ref):
    n = D_FF // FF_CHUNK
    cols = lambda c: slice(c * FF_CHUNK, (c + 1) * FF_CHUNK)
    up_cols = lambda c: slice(D_FF + c * FF_CHUNK, D_FF + (c + 1) * FF_CHUNK)
    up = lambda c: (_dot(xn, wi_ref[:, cols(c)]), _dot(xn, wi_ref[:, up_cols(c)]))
    acc = None
    gu = up(0)
    for c in range(n):
        gu_next = up(c + 1) if c + 1 < n else None
        g, u = gu
        a = (g * jax.nn.sigmoid(g) * u).astype(BF16)
        d = _dot(a, wo_ref[cols(c), :])
        acc = d if acc is None else acc + d
        gu = gu_next
    return acc


def _const_spec(shape, grid_rank, layer=None):
    index = (0,) * len(shape) if layer is None else (layer,) + (0,) * (len(shape) - 1)
    block = tuple(shape) if layer is None else (None,) + tuple(shape[1:])
    if grid_rank == 1:
        index_map = lambda i: index
    elif grid_rank == 2:
        index_map = lambda i, j: index
    else:
        index_map = lambda i, j, k: index
    return pl.BlockSpec(block, index_map, pipeline_mode=pl.Buffered(1))


def _param_specs(params, grid_rank, layer):
    return [_const_spec(a.shape, grid_rank, layer if a.ndim == 3 else None) for a in params]


def _row_spec(tm, width):
    return pl.BlockSpec((tm, width), lambda i: (i, 0))


def _rope_kernel(pos_ref, inv_ref, lo_ref, hi_ref, c_ref, slo_ref, shi_ref):
    ang = pos_ref[...].astype(F32) * inv_ref[...]
    sin = jnp.sin(ang)
    c_ref[...] = jnp.cos(ang)
    slo_ref[...] = sin * lo_ref[...]
    shi_ref[...] = sin * hi_ref[...]


def _rope_tables(positions):
    t = positions.size
    tm = min(t, 1024)
    groups = ROPE_TILE // QK_ROPE_DIM
    inv = ROPE_THETA ** (-jnp.arange(0, QK_ROPE_DIM, 2, dtype=F32) / QK_ROPE_DIM)
    ones_h = jnp.ones((ROPE_HALF,), F32)
    zeros_h = jnp.zeros((ROPE_HALF,), F32)
    inv_row = jnp.tile(inv, 2 * groups)[None, :]
    lo = jnp.tile(jnp.concatenate([-ones_h, zeros_h]), groups)[None, :]
    hi = jnp.tile(jnp.concatenate([zeros_h, ones_h]), groups)[None, :]
    row = _const_spec((1, ROPE_TILE), 1)
    return pl.pallas_call(
        _rope_kernel,
        grid=(t // tm,),
        in_specs=[pl.BlockSpec((tm, 1), lambda i: (i, 0)), row, row, row],
        out_specs=[_row_spec(tm, ROPE_TILE)] * 3,
        out_shape=[jax.ShapeDtypeStruct((t, ROPE_TILE), F32)] * 3,
        compiler_params=pltpu.CompilerParams(dimension_semantics=("parallel",)),
        name="rope_tables",
    )(positions.reshape(t, 1), inv_row, lo, hi)


def _rotate(x, c, s_lo, s_hi):
    ahead = pltpu.roll(x, shift=ROPE_TILE - ROPE_HALF, axis=1)
    behind = pltpu.roll(x, shift=ROPE_HALF, axis=1)
    return x * c + ahead * s_lo + behind * s_hi


def _silu(x):
    return x * jax.nn.sigmoid(x)


def _pre_kernel(x_ref, n1_ref, wi_ref, wo_ref, nm_ref, win_ref, qn_ref, kn_ref,
                wqn_ref, wqr_ref, wk_ref, wv_ref, cw_ref, cb_ref, c_ref, slo_ref, shi_ref,
                h_ref, zg_ref, xbc_ref, dt_ref, q_ref, k_ref, v_ref, raw_ref, hist_ref,
                *, tiles_per_seq):
    i = pl.program_id(0)

    @pl.when(i == 0)
    def _():
        raw_ref[...] = jnp.zeros_like(raw_ref)
        hist_ref[...] = jnp.zeros_like(hist_ref)

    prev = raw_ref[...]
    seq_start = lax.rem(i - 1, tiles_per_seq) == 0
    hist = jnp.where(seq_start, 0.0, hist_ref[...])
    xpad = jnp.concatenate([hist, prev], axis=0)
    conv = cb_ref[...] + cw_ref[CONV_WIDTH - 1:CONV_WIDTH, :] * prev
    for back in range(1, CONV_WIDTH):
        shifted = pltpu.roll(xpad, shift=back, axis=0)[CONV_HIST:, :]
        conv = conv + cw_ref[CONV_WIDTH - 1 - back:CONV_WIDTH - back, :] * shifted
    xbc_ref[...] = _silu(conv)
    hist_ref[...] = prev[prev.shape[0] - CONV_HIST:, :]

    x = x_ref[...]
    xn = _rms(x, n1_ref[...]).astype(BF16)
    h1 = x + 0.5 * _ffn(xn, wi_ref, wo_ref)
    h_ref[...] = h1
    u = _rms(h1, nm_ref[...]).astype(BF16)
    tail = _dot(u, win_ref[:, _C_QA:_C_END])
    misc = tail[:, _C_DT - _C_QA:]
    dt_ref[...] = misc
    qn = _rms(tail[:, :_C_KVA - _C_QA], qn_ref[...]).astype(BF16)
    kvn = _rms(tail[:, _C_KVA - _C_QA:_C_DT - _C_QA], kn_ref[...]).astype(BF16)
    c, s_lo, s_hi = c_ref[...], slo_ref[...], shi_ref[...]
    lane = lax.broadcasted_iota(jnp.int32, (1, ROPE_TILE), 1)
    first_group = lane < QK_ROPE_DIM
    kr = jnp.where(first_group, pltpu.roll(_rotate(misc, c, s_lo, s_hi), shift=QK_ROPE_DIM, axis=1),
                   0.0).astype(BF16)
    q_scale = SOFTMAX_SCALE * LOG2_E
    q_nope = _dot(qn, wqn_ref[...]) * q_scale
    k_nope = _dot(kvn, wk_ref[...])
    q_rope = _dot(qn, wqr_ref[...])
    for pair in range(MLA_HEADS // 2):
        tile = slice(pair * ROPE_TILE, (pair + 1) * ROPE_TILE)
        rot = _rotate(q_rope[:, tile], c, s_lo, s_hi) * q_scale
        for sub, part in enumerate((jnp.where(first_group, rot, 0.0),
                                    jnp.where(first_group, pltpu.roll(rot, shift=QK_ROPE_DIM, axis=1), 0.0))):
            base = (2 * pair + sub) * QK_PAD
            q_ref[:, base + QK_NOPE_DIM:base + QK_PAD] = part.astype(BF16)
    for h in range(MLA_HEADS):
        nope = slice(h * QK_NOPE_DIM, (h + 1) * QK_NOPE_DIM)
        q_ref[:, h * QK_PAD:h * QK_PAD + QK_NOPE_DIM] = q_nope[:, nope].astype(BF16)
        k_ref[:, h * QK_PAD:h * QK_PAD + QK_NOPE_DIM] = k_nope[:, nope].astype(BF16)
        k_ref[:, h * QK_PAD + QK_NOPE_DIM:(h + 1) * QK_PAD] = kr
    v_ref[0] = lax.dot_general(wv_ref[...], kvn, (((1,), (1,)), ((), ())),
                               preferred_element_type=F32).astype(BF16)
    zg_ref[...] = _silu(_dot(u, win_ref[:, _C_Z:_C_XBC]))
    raw_ref[...] = _dot(u, win_ref[:, _C_XBC:_C_QA])


def _pre_call(h, params, rope, seq, layer):
    t = h.shape[0]
    tm = ATTN_TK
    n = t // tm
    cur = lambda i: (jnp.minimum(i, n - 1), 0)
    prev = lambda i: (jnp.maximum(i - 1, 0), 0)
    in_specs = ([pl.BlockSpec((tm, D_MODEL), cur)] + _param_specs(params, 1, layer)
                + [pl.BlockSpec((tm, ROPE_TILE), cur)] * len(rope))
    outs = [(D_MODEL, F32, cur), (D_SSM, F32, cur), (CONV_DIM, F32, prev), (DT_PAD, F32, cur),
            (MLA_HEADS * QK_PAD, BF16, cur), (MLA_HEADS * QK_PAD, BF16, cur)]
    return pl.pallas_call(
        functools.partial(_pre_kernel, tiles_per_seq=seq // tm),
        grid=(n + 1,),
        in_specs=in_specs,
        out_specs=[pl.BlockSpec((tm, w), im) for w, _, im in outs]
        + [pl.BlockSpec((1, D_MLA, tm), lambda i: (jnp.minimum(i, n - 1), 0, 0))],
        out_shape=[jax.ShapeDtypeStruct((t, w), d) for w, d, _ in outs]
        + [jax.ShapeDtypeStruct((n, D_MLA, tm), BF16)],
        scratch_shapes=[pltpu.VMEM((tm, CONV_DIM), F32), pltpu.VMEM((CONV_HIST, CONV_DIM), F32)],
        compiler_params=pltpu.CompilerParams(
            dimension_semantics=("arbitrary",), vmem_limit_bytes=V7X_VMEM_LIMIT_BYTES),
        name="pre",
    )(h, *params, *rope)


def _split_dot(v, m, parts, left=False):
    out = None
    rem = v
    for _ in range(parts):
        piece = rem.astype(BF16)
        rem = rem - piece.astype(F32)
        d = _dot(m, piece) if left else _dot(piece, m)
        out = d if out is None else out + d
    return out


def _softplus(x):
    return jnp.maximum(x, 0.0) + jnp.log1p(jnp.exp(-jnp.abs(x)))


def _ssd_block(xbc_ref, zg_ref, dt_ref, dtb_ref, alog_ref, dskip_ref, nw_ref,
               tri_ref, exp_ref, y_ref, state_ref):
    q = SSD_BLOCK
    hp = SSD_HEADS_PER_GROUP * SSD_HEAD_DIM
    xbc = xbc_ref[...]
    xs = xbc[:, :D_SSM]

    lane = lax.broadcasted_iota(jnp.int32, (1, DT_PAD), 1)
    head_lane = lane < SSD_HEADS
    dt = jnp.where(head_lane, _softplus(dt_ref[...] + dtb_ref[...]), 0.0)
    a_neg = jnp.where(head_lane, -jnp.exp(alog_ref[...]), 0.0)
    a_cum = _split_dot(dt * (a_neg * LOG2_E), tri_ref[...], 3, left=True)
    a_cum_t = a_cum.T
    a_last = a_cum[q - 1:q, :]
    to_end = jnp.exp2(a_last - a_cum)
    from_start = jnp.exp2(a_cum)

    def expand(v):
        hi = v.astype(BF16)
        lo = (v - hi.astype(F32)).astype(BF16)
        return _dot(jnp.concatenate([hi, lo], axis=1), exp_ref[...])

    dt_e = expand(dt)
    w_e = expand(dt * to_end)
    fs_e = expand(from_start)
    xdt = (xs * dt_e).astype(BF16)
    xw = (xs * w_e).astype(BF16)

    row = lax.broadcasted_iota(jnp.int32, (q, q), 0)
    col = lax.broadcasted_iota(jnp.int32, (q, q), 1)
    causal = col <= row
    lane_pair = lax.broadcasted_iota(jnp.int32, (1, 2 * SSD_HEAD_DIM), 1)
    first_of_pair = lane_pair < SSD_HEAD_DIM

    zg = zg_ref[...]
    for g in range(SSD_GROUPS):
        yield
        b_g = xbc[:, D_SSM + g * SSD_STATE:D_SSM + (g + 1) * SSD_STATE]
        c_g = xbc[:, D_SSM + (SSD_GROUPS + g) * SSD_STATE:D_SSM + (SSD_GROUPS + g + 1) * SSD_STATE]
        b_bf = b_g.astype(BF16)
        c_bf = c_g.astype(BF16)
        cb = lax.dot_general(c_bf, b_bf, (((1,), (1,)), ((), ())), preferred_element_type=F32)
        gsl = slice(g * hp, (g + 1) * hp)
        state = state_ref[g]
        y_g = _dot(c_bf, state.astype(BF16)) * fs_e[:, gsl] + xs[:, gsl] * dskip_ref[:, gsl]
        pieces = []
        for pair in range(SSD_HEADS_PER_GROUP // 2):
            outs = []
            for sub in range(2):
                h = g * SSD_HEADS_PER_GROUP + 2 * pair + sub
                seg = a_cum[:, h:h + 1] - a_cum_t[h:h + 1, :]
                m = (cb * jnp.exp2(jnp.where(causal, seg, NEG_BIG))).astype(BF16)
                psl = slice(g * hp + pair * 2 * SSD_HEAD_DIM, g * hp + (pair + 1) * 2 * SSD_HEAD_DIM)
                outs.append(_dot(m, xdt[:, psl]))
            pieces.append(jnp.where(first_of_pair, outs[0], outs[1]))
        y_g = y_g + jnp.concatenate(pieces, axis=-1)
        y_g = y_g * zg[:, gsl]
        y_g = y_g * lax.rsqrt(jnp.mean(y_g * y_g, axis=-1, keepdims=True) + EPS) * nw_ref[:, gsl]
        y_ref[:, gsl] = y_g.astype(y_ref.dtype)
        state_ref[g] = state * fs_e[q - 1:q, gsl] + _dot(b_g.T.astype(BF16), xw[:, gsl])


def _mix_kernel(q_ref, k_ref, vt_ref, xbc_ref, zg_ref, dt_ref, dtb_ref, alog_ref, dskip_ref, nw_ref,
                tri_ref, exp_ref, o_ref, y_ref, m_ref, l_ref, acc_ref, state_ref, s_ref):
    @pl.when((pl.program_id(1) == 0) & (pl.program_id(2) == 0))
    def _():
        state_ref[...] = jnp.zeros_like(state_ref)

    ssd = _ssd_block(xbc_ref, zg_ref, dt_ref, dtb_ref, alog_ref, dskip_ref, nw_ref, tri_ref, exp_ref,
                     y_ref, state_ref)
    _attn_body(q_ref, k_ref, vt_ref, o_ref, m_ref, l_ref, acc_ref, ssd, s_ref)


def _attn_body(q_ref, k_ref, vt_ref, o_ref, m_ref, l_ref, acc_ref, side_work, s_ref):
    tq, tk = q_ref.shape[0], ATTN_TK
    qi = pl.program_id(2)
    heads = range(ATTN_HEADS_PER_STEP)
    ones = jnp.ones((ATTN_ONES_ROWS, tk), BF16)

    def qk(g, j, tiles, q_lo):
        start = pl.multiple_of(j * tk, tk)
        k = k_ref[pl.ds(start, tiles * tk), g * QK_PAD:(g + 1) * QK_PAD]
        q = q_ref[q_lo:, g * QK_PAD:(g + 1) * QK_PAD]
        return lax.dot_general(k, q, (((1,), (1,)), ((), ())), preferred_element_type=F32)

    def fold(g, s, j, tiles, q_lo, mask, first):
        if mask is not None:
            s = jnp.where(mask, s, NEG_BIG)
        m_blk = jnp.max(s, axis=0, keepdims=True)
        if first:
            m_new = m_blk
        else:
            m_old = m_ref[g, :, q_lo:]
            m_new = jnp.maximum(m_old, m_blk)
            alpha = jnp.exp2(m_old - m_new)
        p = jnp.exp2(s - m_new).astype(BF16)
        pv = None
        for i in range(tiles):
            vt = jnp.concatenate([vt_ref[j + i, g * V_HEAD_DIM:(g + 1) * V_HEAD_DIM, :], ones], axis=0)
            d = _dot(vt, p[i * tk:(i + 1) * tk, :])
            pv = d if pv is None else pv + d
        p_sum = pv[V_HEAD_DIM:V_HEAD_DIM + 1, :]
        pv = pv[:V_HEAD_DIM, :]
        if first:
            l_ref[g, :, q_lo:] = p_sum
            acc_ref[g, :, q_lo:] = pv
        else:
            l_ref[g, :, q_lo:] = alpha * l_ref[g, :, q_lo:] + p_sum
            acc_ref[g, :, q_lo:] = alpha * acc_ref[g, :, q_lo:] + pv
        m_ref[g, :, q_lo:] = m_new

    def sweep(j, tiles, q_lo, mask, first):
        rows = tiles * tk
        for g in heads:
            s_ref[g, :rows, q_lo:] = qk(g, j, tiles, q_lo)
        for g in heads:
            fold(g, s_ref[g, :rows, q_lo:], j, tiles, q_lo, mask, first)

    for d in range(tq // tk):
        q_lo = d * tk
        key = lax.broadcasted_iota(jnp.int32, (tk, tq - q_lo), 0)
        query = lax.broadcasted_iota(jnp.int32, (tk, tq - q_lo), 1)
        next(side_work, None)
        sweep(qi * (tq // tk) + d, 1, q_lo, key // CHUNK <= query // CHUNK, first=(d == 0))
    for _ in side_work:
        pass

    sweeps = qi * (tq // tk) // ATTN_LOOP_TILES
    odd = sweeps % 2

    @pl.when(odd == 1)
    def _():
        sweep(0, ATTN_LOOP_TILES, 0, None, first=False)

    @pl.loop(0, sweeps // 2)
    def _(trip):
        j = (odd + 2 * trip) * ATTN_LOOP_TILES
        sweep(j, ATTN_LOOP_TILES, 0, None, first=False)
        sweep(j + ATTN_LOOP_TILES, ATTN_LOOP_TILES, 0, None, first=False)

    for g in heads:
        o_ref[:, g * V_HEAD_DIM:(g + 1) * V_HEAD_DIM] = (acc_ref[g] / l_ref[g]).T.astype(o_ref.dtype)


def _mix_call(q, k, vt, xbc, zg, dtm, params, batch, seq, layer):
    tq = min(seq, ATTN_TQ)
    nq = seq // tq
    g = ATTN_HEADS_PER_STEP
    groups = MLA_HEADS // g
    nblk = seq // SSD_BLOCK
    hp = SSD_HEADS_PER_GROUP * SSD_HEAD_DIM
    assert (tq // ATTN_TK) % ATTN_LOOP_TILES == 0 and seq % tq == 0 and MLA_HEADS % g == 0
    assert groups * nq == nblk
    ssd_blk = lambda w: pl.BlockSpec((SSD_BLOCK, w), lambda b, h, i: (b * nblk + h * nq + i, 0))
    return pl.pallas_call(
        _mix_kernel,
        grid=(batch, groups, nq),
        in_specs=[pl.BlockSpec((tq, g * QK_PAD), lambda b, h, i: (b * nq + i, h)),
                  pl.BlockSpec((seq, g * QK_PAD), lambda b, h, i: (b, h)),
                  pl.BlockSpec((seq // ATTN_TK, g * V_HEAD_DIM, ATTN_TK), lambda b, h, i: (b, h, 0)),
                  ssd_blk(CONV_DIM), ssd_blk(D_SSM), ssd_blk(DT_PAD)] + _param_specs(params, 3, layer),
        out_specs=[pl.BlockSpec((tq, g * V_HEAD_DIM), lambda b, h, i: (b * nq + i, h)), ssd_blk(D_SSM)],
        out_shape=[jax.ShapeDtypeStruct((batch * seq, D_MLA), BF16),
                   jax.ShapeDtypeStruct((batch * seq, D_SSM), BF16)],
        scratch_shapes=[pltpu.VMEM((g, 1, tq), F32), pltpu.VMEM((g, 1, tq), F32),
                        pltpu.VMEM((g, V_HEAD_DIM, tq), F32),
                        pltpu.VMEM((SSD_GROUPS, SSD_STATE, hp), F32),
                        pltpu.VMEM((g, ATTN_LOOP_TILES * ATTN_TK, tq), F32)],
        compiler_params=pltpu.CompilerParams(
            dimension_semantics=("arbitrary", "arbitrary", "arbitrary"),
            vmem_limit_bytes=V7X_VMEM_LIMIT_BYTES),
        name="mix",
    )(q, k, vt, xbc, zg, dtm, *params)


def _post_kernel(h_ref, ys_ref, ym_ref, p_ref, wmix_ref, n2_ref, wi_ref, wo_ref,
                 np_ref, wgate_ref, wproj_ref, nf_ref, o_ref, *, final):
    h2 = (h_ref[...] + _dot(ys_ref[...], wmix_ref[:D_SSM, :]) + _dot(ym_ref[...], wmix_ref[D_SSM:, :]))
    xn = _rms(h2, n2_ref[...]).astype(BF16)
    h3 = h2 + 0.5 * _ffn(xn, wi_ref, wo_ref)
    gate = jax.nn.sigmoid(_dot(_rms(h3, np_ref[...]).astype(BF16), wgate_ref[...]))
    h4 = h3 + gate * _dot(p_ref[...].astype(BF16), wproj_ref[...])
    if final:
        h4 = _rms(h4, nf_ref[...])
    o_ref[...] = h4


def _post_call(h, ys, ym, p, params, layer, final):
    t = h.shape[0]
    tm = min(t, TOKEN_TILE)
    in_specs = ([_row_spec(tm, D_MODEL), _row_spec(tm, D_SSM), _row_spec(tm, D_MLA),
                 pl.BlockSpec((None, tm, PLE_DIM), lambda i: (layer, i, 0))]
                + _param_specs(params, 1, layer))
    return pl.pallas_call(
        functools.partial(_post_kernel, final=final),
        grid=(t // tm,),
        in_specs=in_specs,
        out_specs=_row_spec(tm, D_MODEL),
        out_shape=jax.ShapeDtypeStruct((t, D_MODEL), F32),
        compiler_params=pltpu.CompilerParams(
            dimension_semantics=("parallel",), vmem_limit_bytes=V7X_VMEM_LIMIT_BYTES),
        name="post",
    )(h, ys, ym, p, *params)


def _pad_last(w, width):
    return jnp.pad(w, [(0, 0)] * (w.ndim - 1) + [(0, width - w.shape[-1])])


def _layout_in_proj(w):
    o = 0
    w_z = w[..., o:o + D_SSM]; o += D_SSM
    w_xbc = w[..., o:o + CONV_DIM]; o += CONV_DIM
    w_dt = w[..., o:o + SSD_HEADS]; o += SSD_HEADS
    w_qa = w[..., o:o + Q_LORA_RANK]; o += Q_LORA_RANK
    w_kva = w[..., o:o + KV_LORA_RANK]; o += KV_LORA_RANK
    w_kr = w[..., o:o + QK_ROPE_DIM]
    return jnp.concatenate([
        w_z, w_xbc, w_qa, w_kva, _pad_last(w_dt, DT_PAD - QK_ROPE_DIM), w_kr], axis=-1).astype(BF16)


def _layout_q_b(w):
    depth = w.shape[0]
    w = w.reshape(depth, Q_LORA_RANK, MLA_HEADS, QK_NOPE_DIM + QK_ROPE_DIM)
    return (w[..., :QK_NOPE_DIM].reshape(depth, Q_LORA_RANK, MLA_HEADS * QK_NOPE_DIM).astype(BF16),
            w[..., QK_NOPE_DIM:].reshape(depth, Q_LORA_RANK, MLA_HEADS * QK_ROPE_DIM).astype(BF16))


def _layout_kv_b(w):
    depth = w.shape[0]
    w = w.reshape(depth, KV_LORA_RANK, MLA_HEADS, QK_NOPE_DIM + V_HEAD_DIM)
    wk = w[..., :QK_NOPE_DIM].reshape(depth, KV_LORA_RANK, MLA_HEADS * QK_NOPE_DIM)
    wv_t = jnp.swapaxes(w[..., QK_NOPE_DIM:].reshape(depth, KV_LORA_RANK, D_MLA), 1, 2)
    return wk.astype(BF16), wv_t.astype(BF16)


def _rows(v):
    return v.reshape(v.shape[0], 1, -1).astype(F32)


def kernel(x, p, positions, ffn1_norm, ffn1_w_in, ffn1_w_out, mix_norm, w_in_mix, conv_w, conv_b,
           dt_bias, a_log, d_skip, ssd_norm, q_a_norm, w_q_b, kv_a_norm, w_kv_b, w_out_mix,
           ffn2_norm, ffn2_w_in, ffn2_w_out, ple_norm, w_ple_gate, w_ple_proj, final_norm):
    batch, seq, _ = x.shape
    depth = p.shape[0]
    t = batch * seq
    assert seq % SSD_BLOCK == 0 and seq % min(seq, ATTN_TQ) == 0 and t % TOKEN_TILE == 0

    rope = _rope_tables(positions)
    q_idx = jnp.arange(SSD_BLOCK)
    tri = (q_idx[None, :] <= q_idx[:, None]).astype(BF16)
    expand = (jnp.arange(D_SSM)[None, :] // SSD_HEAD_DIM == jnp.arange(DT_PAD)[:, None]).astype(BF16)
    expand = jnp.concatenate([expand, expand], axis=0)

    wqn, wqr = _layout_q_b(w_q_b)
    wk, wv = _layout_kv_b(w_kv_b)
    pre_params = [_rows(ffn1_norm), ffn1_w_in.astype(BF16), ffn1_w_out.astype(BF16), _rows(mix_norm),
                  _layout_in_proj(w_in_mix), _rows(q_a_norm), _rows(kv_a_norm), wqn, wqr, wk, wv,
                  conv_w.astype(F32), _rows(conv_b)]
    ssd_params = [_pad_last(_rows(dt_bias), DT_PAD), _pad_last(_rows(a_log), DT_PAD),
                  _rows(jnp.repeat(d_skip, SSD_HEAD_DIM, axis=-1)), _rows(ssd_norm), tri, expand]
    post_params = [w_out_mix.astype(BF16), _rows(ffn2_norm), ffn2_w_in.astype(BF16),
                   ffn2_w_out.astype(BF16), _rows(ple_norm), w_ple_gate.astype(BF16),
                   w_ple_proj.astype(BF16), final_norm.reshape(1, -1).astype(F32)]
    p = p.reshape(depth, t, PLE_DIM)

    h = x.reshape(t, D_MODEL)
    for i in range(depth):
        h, zg, xbc, dtm, q, k, v = _pre_call(h, pre_params, rope, seq, i)
        y_mla, y_ssd = _mix_call(q, k, v, xbc, zg, dtm, ssd_params, batch, seq, i)
        h = _post_call(h, y_ssd, y_mla, p, post_params, i, final=(i == depth - 1))
    return h.reshape(batch, seq, D_MODEL)
```
